```python
import jax, jax.numpy as jnp
from jax import lax
import numpy as np

D_MODEL = 1024
BATCH = 8
SEQ = 4096
DEPTH = 2

CHUNK = 64
GLA_HEADS = 4
KEY_WIDTH = D_MODEL // 2
VAL_WIDTH = D_MODEL
HEAD_K = KEY_WIDTH // GLA_HEADS
HEAD_V = VAL_WIDTH // GLA_HEADS
GATE_RANK = 16
GATE_TAU = 16.0
CONV_CH = D_MODEL
CONV_WIDTH = 3
FFN_HIDDEN = -(-8 * D_MODEL // (3 * 256)) * 256
IN_WIDTH = 2 * KEY_WIDTH + 2 * VAL_WIDTH + GATE_RANK + 3 * CONV_CH + 2 * D_MODEL
NORM_EPS = 1e-6

kernel_name = "gla_shortconv_gated_hybrid"


def _split_points():
    sizes = (KEY_WIDTH, KEY_WIDTH, VAL_WIDTH, VAL_WIDTH, GATE_RANK,
             CONV_CH, CONV_CH, CONV_CH, D_MODEL, D_MODEL)
    return tuple(int(v) for v in np.cumsum(sizes)[:-1])


def rmsnorm(x, g):
    xf = x.astype(jnp.float32)
    y = xf * lax.rsqrt(jnp.mean(xf * xf, axis=-1, keepdims=True) + NORM_EPS)
    return (y * g.astype(jnp.float32)).astype(x.dtype)


def gla_chunk_causal(q, k, v, log_a):
    b_sz, s_len = q.shape[0], q.shape[1]
    n_chunks = s_len // CHUNK

    def to_chunks(t):
        return t.astype(jnp.float32).reshape(b_sz, n_chunks, CHUNK, GLA_HEADS, t.shape[-1]).transpose(1, 0, 3, 2, 4)

    qc, kc, vc, lc = to_chunks(q), to_chunks(k), to_chunks(v), to_chunks(log_a)
    cum = jnp.cumsum(lc, axis=3)
    cum_end = cum[:, :, :, -1:, :]
    kd = kc * jnp.exp(cum_end - cum)
    gamma = jnp.exp(cum_end[:, :, :, 0, :])

    def step(state, inp):
        q_i, kd_i, v_i, g_i = inp
        state = g_i[..., None] * state + jnp.einsum('bhlk,bhlv->bhkv', kd_i, v_i)
        o_i = jnp.einsum('bhlk,bhkv->bhlv', q_i, state)
        return state, o_i

    s0 = jnp.zeros((b_sz, GLA_HEADS, HEAD_K, HEAD_V), jnp.float32)
    _, o = lax.scan(step, s0, (qc, kd, vc, gamma))
    return o.transpose(1, 0, 3, 2, 4).reshape(b_sz, s_len, GLA_HEADS, HEAD_V)


def causal_depthwise_conv(u, w, bias):
    rhs = w.astype(u.dtype)[:, None, :]
    y = lax.conv_general_dilated(u, rhs, window_strides=(1,), padding=[(CONV_WIDTH - 1, 0)],
                                 dimension_numbers=('NWC', 'WIO', 'NWC'),
                                 feature_group_count=u.shape[-1])
    return y + bias.astype(u.dtype)


def setup_inputs(seed: int = 0) -> dict:
    key = jax.random.key(seed)
    ks = jax.random.split(key, 20)

    def nrm(k, shape, scale):
        return jax.random.normal(k, shape, jnp.float32) * scale

    def gain(k, shape):
        return 1.0 + 0.02 * jax.random.normal(k, shape, jnp.float32)

    return {
        "x": nrm(ks[0], (BATCH, SEQ, D_MODEL), 1.0),
        "norm1_g": gain(ks[1], (DEPTH, D_MODEL)),
        "w_in": nrm(ks[2], (DEPTH, D_MODEL, IN_WIDTH), D_MODEL ** -0.5),
        "w_fg2": nrm(ks[3], (DEPTH, GATE_RANK, KEY_WIDTH), GATE_RANK ** -0.5),
        "b_fg": nrm(ks[4], (DEPTH, KEY_WIDTH), 0.01),
        "gla_norm_g": gain(ks[5], (DEPTH, HEAD_V)),
        "w_oa": nrm(ks[6], (DEPTH, VAL_WIDTH, D_MODEL), VAL_WIDTH ** -0.5),
        "conv_w": nrm(ks[7], (DEPTH, CONV_WIDTH, CONV_CH), CONV_WIDTH ** -0.5),
        "conv_b": nrm(ks[8], (DEPTH, CONV_CH), 0.01),
        "w_ob": nrm(ks[9], (DEPTH, CONV_CH, D_MODEL), CONV_CH ** -0.5),
        "w_o": nrm(ks[10], (DEPTH, D_MODEL, D_MODEL), D_MODEL ** -0.5),
        "norm2_g": gain(ks[11], (DEPTH, D_MODEL)),
        "w_ffn_gate": nrm(ks[12], (DEPTH, D_MODEL, FFN_HIDDEN), D_MODEL ** -0.5),
        "w_ffn_up": nrm(ks[13], (DEPTH, D_MODEL, FFN_HIDDEN), D_MODEL ** -0.5),
        "w_ffn_down": nrm(ks[14], (DEPTH, FFN_HIDDEN, D_MODEL), FFN_HIDDEN ** -0.5),
        "final_g": gain(ks[15], (D_MODEL,)),
    }


def reference(x, norm1_g, w_in, w_fg2, b_fg, gla_norm_g, w_oa, conv_w, conv_b, w_ob, w_o,
              norm2_g, w_ffn_gate, w_ffn_up, w_ffn_down, final_g):
    b_sz, s_len, _ = x.shape
    split_pts = _split_points()
    for l in range(DEPTH):
        h = rmsnorm(x, norm1_g[l])
        proj = jnp.einsum('bsd,de->bse', h, w_in[l])
        q, k, v, r, fz, gb_in, gc_in, cx, ga, gb = jnp.split(proj, split_pts, axis=-1)

        fg = jnp.einsum('bsr,rk->bsk', fz, w_fg2[l]) + b_fg[l]
        log_a = jax.nn.log_sigmoid(fg.astype(jnp.float32)) / GATE_TAU
        qh = q.reshape(b_sz, s_len, GLA_HEADS, HEAD_K) * (HEAD_K ** -0.5)
        kh = k.reshape(b_sz, s_len, GLA_HEADS, HEAD_K)
        vh = v.reshape(b_sz, s_len, GLA_HEADS, HEAD_V)
        ah = log_a.reshape(b_sz, s_len, GLA_HEADS, HEAD_K)
        o = gla_chunk_causal(qh, kh, vh, ah)
        o = o * lax.rsqrt(jnp.mean(o * o, axis=-1, keepdims=True) + NORM_EPS) * gla_norm_g[l].astype(jnp.float32)
        o = o.reshape(b_sz, s_len, VAL_WIDTH).astype(x.dtype) * jax.nn.silu(r)
        y_a = jnp.einsum('bsv,vd->bsd', o, w_oa[l])

        conv = causal_depthwise_conv(gc_in * cx, conv_w[l], conv_b[l])
        y_b = jnp.einsum('bsc,cd->bsd', gb_in * conv, w_ob[l])

        mix = jax.nn.sigmoid(ga) * y_a + jax.nn.sigmoid(gb) * y_b
        x = x + jnp.einsum('bsd,de->bse', mix, w_o[l])

        h2 = rmsnorm(x, norm2_g[l])
        hid = jax.nn.silu(jnp.einsum('bsd,df->bsf', h2, w_ffn_gate[l])) * jnp.einsum('bsd,df->bsf', h2, w_ffn_up[l])
        x = x + jnp.einsum('bsf,fd->bsd', hid, w_ffn_down[l])
    return rmsnorm(x, final_g)
```

```python
import functools

import jax
import jax.numpy as jnp
from jax import lax
from jax.experimental import pallas as pl
from jax.experimental.pallas import tpu as pltpu

D_MODEL = 1024
CHUNK = 64
GLA_HEADS = 4
KEY_WIDTH = 512
VAL_WIDTH = 1024
HEAD_K = KEY_WIDTH // GLA_HEADS
HEAD_V = VAL_WIDTH // GLA_HEADS
GATE_RANK = 16
GATE_TAU = 16.0
CONV_CH = 1024
CONV_WIDTH = 3
NORM_EPS = 1e-6

V7X_LANES = 128
V7X_SUBLANES = 8
V7X_MXU_DIM = 256
V7X_VMEM_BYTES = 64 * 1024 * 1024

_A_WIDTH = 2 * KEY_WIDTH + 2 * VAL_WIDTH
_B_WIDTH = 3 * CONV_CH + 2 * D_MODEL
_FZ_PAD = V7X_LANES

MIX_ROWS = 512
FFN_ROWS = 512
CUM_ROWS = V7X_MXU_DIM


def _vmem_limit(resident_bytes, tile_bytes):
    want = resident_bytes + 2 * tile_bytes + 24 * 1024 * 1024
    return int(min(want, V7X_VMEM_BYTES - 6 * 1024 * 1024))


def _rmsnorm(x, g):
    ms = jnp.mean(x * x, axis=-1, keepdims=True)
    return x * lax.rsqrt(ms + NORM_EPS) * g


def _log_sigmoid(x):
    return jnp.minimum(x, 0.0) - jnp.log1p(jnp.exp(-jnp.abs(x)))


def _dot(a, b):
    return jnp.dot(a, b, preferred_element_type=jnp.float32)


def _mixer_kernel(x_ref, g1_ref, wa_ref, wfz_ref, wb_ref, wfg2_ref, bfg_ref,
                  gn_ref, woa_ref, cw_ref, cb_ref, wob_ref, wo_ref,
                  out_ref,
                  state_ref, q_ref, kd_ref, v_ref, lsum_ref, o_ref, u_ref):
    rows = x_ref.shape[1]
    bf16 = jnp.bfloat16

    @pl.when(pl.program_id(1) == 0)
    def _():
        state_ref[...] = jnp.zeros_like(state_ref)
        u_ref[pl.ds(0, V7X_SUBLANES), :] = jnp.zeros((V7X_SUBLANES, CONV_CH), jnp.float32)

    x = x_ref[0]
    h = _rmsnorm(x, g1_ref[...]).astype(bf16)

    q = _dot(h, wa_ref[:, 0:KEY_WIDTH]) * (HEAD_K ** -0.5)
    q_ref[...] = q.astype(bf16)
    k = _dot(h, wa_ref[:, KEY_WIDTH:2 * KEY_WIDTH])
    v_ref[...] = _dot(h, wa_ref[:, 2 * KEY_WIDTH:2 * KEY_WIDTH + VAL_WIDTH]).astype(bf16)

    fz = _dot(h, wfz_ref[...]).astype(bf16)
    fg = _dot(fz, wfg2_ref[...]) + bfg_ref[...]
    log_a = _log_sigmoid(fg) * (1.0 / GATE_TAU)

    ri = lax.broadcasted_iota(jnp.int32, (CUM_ROWS, CUM_ROWS), 0)
    ci = lax.broadcasted_iota(jnp.int32, (CUM_ROWS, CUM_ROWS), 1)
    later = jnp.where((ri // CHUNK == ci // CHUNK) & (ci > ri), 1.0, 0.0).astype(bf16)
    la_hi = log_a.astype(bf16)
    la_lo = (log_a - la_hi.astype(jnp.float32)).astype(bf16)
    rev = jnp.concatenate(
        [_dot(later, la_hi[i:i + CUM_ROWS]) + _dot(later, la_lo[i:i + CUM_ROWS])
         for i in range(0, rows, CUM_ROWS)], axis=0)
    kd_ref[...] = (k * jnp.exp(rev)).astype(bf16)
    lsum_ref[...] = rev + log_a

    def chunk_step(c, carry):
        r0 = pl.multiple_of(c * CHUNK, CHUNK)
        gamma = jnp.exp(lsum_ref[pl.ds(r0, 1), :])
        for hd in range(GLA_HEADS):
            ks = slice(hd * HEAD_K, (hd + 1) * HEAD_K)
            vs = slice(hd * HEAD_V, (hd + 1) * HEAD_V)
            kd_c = kd_ref[pl.ds(r0, CHUNK), ks]
            v_c = v_ref[pl.ds(r0, CHUNK), vs]
            upd = lax.dot_general(v_c, kd_c, (((0,), (0,)), ((), ())),
                                  preferred_element_type=jnp.float32)
            st = state_ref[hd] * gamma[:, ks] + upd
            state_ref[hd] = st
            q_c = q_ref[pl.ds(r0, CHUNK), ks]
            o_ref[pl.ds(r0, CHUNK), vs] = lax.dot_general(
                q_c, st.astype(bf16), (((1,), (1,)), ((), ())),
                preferred_element_type=jnp.float32)
        return carry

    lax.fori_loop(0, rows // CHUNK, chunk_step, 0)

    r = _dot(h, wa_ref[:, 2 * KEY_WIDTH + VAL_WIDTH:_A_WIDTH])
    gn = gn_ref[...]
    parts = []
    for hd in range(GLA_HEADS):
        vs = slice(hd * HEAD_V, (hd + 1) * HEAD_V)
        o_h = o_ref[:, vs]
        ms = jnp.mean(o_h * o_h, axis=-1, keepdims=True)
        parts.append(o_h * lax.rsqrt(ms + NORM_EPS) * gn)
    o_n = jnp.concatenate(parts, axis=1)
    y_a = _dot((o_n * (r * jax.nn.sigmoid(r))).astype(bf16), woa_ref[...])

    u = _dot(h, wb_ref[:, CONV_CH:2 * CONV_CH]) * _dot(h, wb_ref[:, 2 * CONV_CH:3 * CONV_CH])
    u_ref[pl.ds(V7X_SUBLANES, rows), :] = u
    u1 = u_ref[pl.ds(V7X_SUBLANES - 1, rows), :]
    u2 = u_ref[pl.ds(V7X_SUBLANES - 2, rows), :]
    cw = cw_ref[...]
    conv = cw[0:1] * u2 + cw[1:2] * u1 + cw[2:3] * u + cb_ref[...]
    u_ref[pl.ds(0, V7X_SUBLANES), :] = u[rows - V7X_SUBLANES:rows]
    y_b = _dot((_dot(h, wb_ref[:, 0:CONV_CH]) * conv).astype(bf16), wob_ref[...])

    g_a = jax.nn.sigmoid(_dot(h, wb_ref[:, 3 * CONV_CH:3 * CONV_CH + D_MODEL]))
    g_b = jax.nn.sigmoid(_dot(h, wb_ref[:, 3 * CONV_CH + D_MODEL:_B_WIDTH]))
    mix = (g_a * y_a + g_b * y_b).astype(bf16)
    out_ref[0] = x + _dot(mix, wo_ref[...])


def _ffn_kernel(x_ref, g2_ref, wg_ref, wu_ref, wd_ref, gf_ref, out_ref, *, final_norm):
    bf16 = jnp.bfloat16
    x = x_ref[...]
    h = _rmsnorm(x, g2_ref[...]).astype(bf16)
    gate = _dot(h, wg_ref[...])
    up = _dot(h, wu_ref[...])
    hid = (gate * jax.nn.sigmoid(gate) * up).astype(bf16)
    y = x + _dot(hid, wd_ref[...])
    if final_norm:
        y = _rmsnorm(y, gf_ref[...])
    out_ref[...] = y


def _resident(shape):
    return pl.BlockSpec(shape, lambda *_: (0,) * len(shape), pipeline_mode=pl.Buffered(1))


def _nbytes(*arrays):
    return sum(a.size * a.dtype.itemsize for a in arrays)


def _mixer(x, g1, wa, wfz, wb, wfg2, bfg, gn, woa, cw, cb, wob, wo):
    b_sz, s_len, _ = x.shape
    rows = MIX_ROWS
    assert s_len % rows == 0 and rows % CUM_ROWS == 0 and CUM_ROWS % CHUNK == 0
    weights = (g1, wa, wfz, wb, wfg2, bfg, gn, woa, cw, cb, wob, wo)
    tile = pl.BlockSpec((1, rows, D_MODEL), lambda b, s: (b, s, 0))
    scratch = [
        pltpu.VMEM((GLA_HEADS, HEAD_V, HEAD_K), jnp.float32),
        pltpu.VMEM((rows, KEY_WIDTH), jnp.bfloat16),
        pltpu.VMEM((rows, KEY_WIDTH), jnp.bfloat16),
        pltpu.VMEM((rows, VAL_WIDTH), jnp.bfloat16),
        pltpu.VMEM((rows, KEY_WIDTH), jnp.float32),
        pltpu.VMEM((rows, VAL_WIDTH), jnp.float32),
        pltpu.VMEM((rows + V7X_SUBLANES, CONV_CH), jnp.float32),
    ]
    scratch_bytes = (GLA_HEADS * HEAD_V * HEAD_K * 4 + 2 * rows * KEY_WIDTH * 2
                     + rows * VAL_WIDTH * 2 + rows * KEY_WIDTH * 4 + rows * VAL_WIDTH * 4
                     + (rows + V7X_SUBLANES) * CONV_CH * 4)
    return pl.pallas_call(
        _mixer_kernel,
        grid=(b_sz, s_len // rows),
        in_specs=[tile] + [_resident(w.shape) for w in weights],
        out_specs=tile,
        out_shape=jax.ShapeDtypeStruct(x.shape, x.dtype),
        scratch_shapes=scratch,
        compiler_params=pltpu.CompilerParams(
            dimension_semantics=("arbitrary", "arbitrary"),
            vmem_limit_bytes=_vmem_limit(_nbytes(*weights) + scratch_bytes,
                                         2 * rows * D_MODEL * 4)),
        name="mixer",
    )(x, *weights)


def _ffn(x2d, g2, wg, wu, wd, gf, final_norm):
    n_tok, _ = x2d.shape
    rows = FFN_ROWS
    assert n_tok % rows == 0
    weights = (g2, wg, wu, wd, gf)
    tile = pl.BlockSpec((rows, D_MODEL), lambda i: (i, 0))
    return pl.pallas_call(
        functools.partial(_ffn_kernel, final_norm=final_norm),
        grid=(n_tok // rows,),
        in_specs=[tile] + [_resident(w.shape) for w in weights],
        out_specs=tile,
        out_shape=jax.ShapeDtypeStruct(x2d.shape, x2d.dtype),
        compiler_params=pltpu.CompilerParams(
            dimension_semantics=("arbitrary",),
            vmem_limit_bytes=_vmem_limit(_nbytes(*weights), 2 * rows * D_MODEL * 4)),
        name="ffn_final" if final_norm else "ffn",
    )(x2d, *weights)


def kernel(x, norm1_g, w_in, w_fg2, b_fg, gla_norm_g, w_oa, conv_w, conv_b, w_ob, w_o,
           norm2_g, w_ffn_gate, w_ffn_up, w_ffn_down, final_g):
    b_sz, s_len, d = x.shape
    depth = w_in.shape[0]
    bf16 = jnp.bfloat16
    fz0 = _A_WIDTH
    fz1 = _A_WIDTH + GATE_RANK
    for l in range(depth):
        wa = w_in[l, :, :fz0].astype(bf16)
        wfz = jnp.pad(w_in[l, :, fz0:fz1], ((0, 0), (0, _FZ_PAD - GATE_RANK))).astype(bf16)
        wb = w_in[l, :, fz1:].astype(bf16)
        wfg2 = jnp.pad(w_fg2[l], ((0, _FZ_PAD - GATE_RANK), (0, 0))).astype(bf16)
        x = _mixer(x, norm1_g[l][None], wa, wfz, wb, wfg2, b_fg[l][None],
                   gla_norm_g[l][None], w_oa[l].astype(bf16), conv_w[l], conv_b[l][None],
                   w_ob[l].astype(bf16), w_o[l].astype(bf16))
        x = _ffn(x.reshape(b_sz * s_len, d), norm2_g[l][None],
                 w_ffn_gate[l].astype(bf16), w_ffn_up[l].astype(bf16),
                 w_ffn_down[l].astype(bf16), final_g[None],
                 final_norm=(l == depth - 1)).reshape(b_sz, s_len, d)
    return x
```

```python
import functools

import jax
import jax.numpy as jnp
from jax import lax
from jax.experimental import pallas as pl
from jax.experimental.pallas import tpu as pltpu

D_MODEL = 1024
CHUNK = 64
GLA_HEADS = 4
KEY_WIDTH = 512
VAL_WIDTH = 1024
HEAD_K = KEY_WIDTH // GLA_HEADS
HEAD_V = VAL_WIDTH // GLA_HEADS
GATE_RANK = 16
GATE_TAU = 16.0
CONV_CH = 1024
CONV_WIDTH = 3
NORM_EPS = 1e-6

V7X_LANES = 128
V7X_SUBLANES = 8
V7X_MXU_DIM = 256
V7X_VMEM_BYTES = 64 * 1024 * 1024

_A_WIDTH = 2 * KEY_WIDTH + 2 * VAL_WIDTH
_B_WIDTH = 3 * CONV_CH + 2 * D_MODEL
_FZ_PAD = V7X_LANES

MIX_ROWS = 512
FFN_ROWS = 512
CUM_ROWS = V7X_MXU_DIM


def _vmem_limit(resident_bytes, tile_bytes):
    want = resident_bytes + 2 * tile_bytes + 24 * 1024 * 1024
    return int(min(want, V7X_VMEM_BYTES - 6 * 1024 * 1024))


def _rmsnorm(x, g):
    ms = jnp.mean(x * x, axis=-1, keepdims=True)
    return x * lax.rsqrt(ms + NORM_EPS) * g


def _log_sigmoid(x):
    return jnp.minimum(x, 0.0) - jnp.log1p(jnp.exp(-jnp.abs(x)))


def _dot(a, b):
    return jnp.dot(a, b, preferred_element_type=jnp.float32)


def _mixer_kernel(x_ref, g1_ref, wa_ref, wfz_ref, wb_ref, wfg2_ref, bfg_ref,
                  gn_ref, woa_ref, cw_ref, cb_ref, wob_ref, wo_ref,
                  out_ref,
                  state_ref, q_ref, kd_ref, v_ref, lsum_ref, o_ref, u_ref):
    rows = x_ref.shape[1]
    bf16 = jnp.bfloat16

    @pl.when(pl.program_id(1) == 0)
    def _():
        state_ref[...] = jnp.zeros_like(state_ref)
        u_ref[pl.ds(0, V7X_SUBLANES), :] = jnp.zeros((V7X_SUBLANES, CONV_CH), jnp.float32)

    x = x_ref[0]
    h = _rmsnorm(x, g1_ref[...]).astype(bf16)

    q = _dot(h, wa_ref[:, 0:KEY_WIDTH]) * (HEAD_K ** -0.5)
    q_ref[...] = q.astype(bf16)
    k = _dot(h, wa_ref[:, KEY_WIDTH:2 * KEY_WIDTH])
    v_ref[...] = _dot(h, wa_ref[:, 2 * KEY_WIDTH:2 * KEY_WIDTH + VAL_WIDTH]).astype(bf16)

    fz = _dot(h, wfz_ref[...]).astype(bf16)
    fg = _dot(fz, wfg2_ref[...]) + bfg_ref[...]
    log_a = _log_sigmoid(fg) * (1.0 / GATE_TAU)

    ri = lax.broadcasted_iota(jnp.int32, (CUM_ROWS, CUM_ROWS), 0)
    ci = lax.broadcasted_iota(jnp.int32, (CUM_ROWS, CUM_ROWS), 1)
    later = jnp.where((ri // CHUNK == ci // CHUNK) & (ci > ri), 1.0, 0.0).astype(bf16)
    la_hi = log_a.astype(bf16)
    la_lo = (log_a - la_hi.astype(jnp.float32)).astype(bf16)
    rev = jnp.concatenate(
        [_dot(later, la_hi[i:i + CUM_ROWS]) + _dot(later, la_lo[i:i + CUM_ROWS])
         for i in range(0, rows, CUM_ROWS)], axis=0)
    kd_ref[...] = (k * jnp.exp(rev)).astype(bf16)
    lsum_ref[...] = rev + log_a

    n_chunks = rows // CHUNK
    chunk_rows = [slice(c * CHUNK, (c + 1) * CHUNK) for c in range(n_chunks)]
    k_cols = [slice(hd * HEAD_K, (hd + 1) * HEAD_K) for hd in range(GLA_HEADS)]
    v_cols = [slice(hd * HEAD_V, (hd + 1) * HEAD_V) for hd in range(GLA_HEADS)]
    kv = [[lax.dot_general(v_ref[rs, v_cols[hd]], kd_ref[rs, k_cols[hd]],
                           (((0,), (0,)), ((), ())), preferred_element_type=jnp.float32)
           for hd in range(GLA_HEADS)] for rs in chunk_rows]

    u = _dot(h, wb_ref[:, CONV_CH:2 * CONV_CH]) * _dot(h, wb_ref[:, 2 * CONV_CH:3 * CONV_CH])
    u_ref[pl.ds(V7X_SUBLANES, rows), :] = u
    u1 = u_ref[pl.ds(V7X_SUBLANES - 1, rows), :]
    u2 = u_ref[pl.ds(V7X_SUBLANES - 2, rows), :]
    cw = cw_ref[...]
    conv = cw[0:1] * u2 + cw[1:2] * u1 + cw[2:3] * u + cb_ref[...]
    u_ref[pl.ds(0, V7X_SUBLANES), :] = u[rows - V7X_SUBLANES:rows]
    y_b = _dot((_dot(h, wb_ref[:, 0:CONV_CH]) * conv).astype(bf16), wob_ref[...])

    states = [state_ref[hd] for hd in range(GLA_HEADS)]
    st_bf = []
    for c in range(n_chunks):
        gamma = jnp.exp(lsum_ref[c * CHUNK:c * CHUNK + 1, :])
        for hd in range(GLA_HEADS):
            states[hd] = states[hd] * gamma[:, k_cols[hd]] + kv[c][hd]
        st_bf.append([s.astype(bf16) for s in states])
    for hd in range(GLA_HEADS):
        state_ref[hd] = states[hd]

    r = _dot(h, wa_ref[:, 2 * KEY_WIDTH + VAL_WIDTH:_A_WIDTH])
    g_a = jax.nn.sigmoid(_dot(h, wb_ref[:, 3 * CONV_CH:3 * CONV_CH + D_MODEL]))
    g_b = jax.nn.sigmoid(_dot(h, wb_ref[:, 3 * CONV_CH + D_MODEL:_B_WIDTH]))

    for c, rs in enumerate(chunk_rows):
        for hd in range(GLA_HEADS):
            o_ref[rs, v_cols[hd]] = lax.dot_general(
                q_ref[rs, k_cols[hd]], st_bf[c][hd], (((1,), (1,)), ((), ())),
                preferred_element_type=jnp.float32)

    gn = gn_ref[...]
    parts = []
    for hd in range(GLA_HEADS):
        o_h = o_ref[:, v_cols[hd]]
        ms = jnp.mean(o_h * o_h, axis=-1, keepdims=True)
        parts.append(o_h * lax.rsqrt(ms + NORM_EPS) * gn)
    o_n = jnp.concatenate(parts, axis=1)
    y_a = _dot((o_n * (r * jax.nn.sigmoid(r))).astype(bf16), woa_ref[...])

    mix = (g_a * y_a + g_b * y_b).astype(bf16)
    out_ref[0] = x + _dot(mix, wo_ref[...])


def _ffn_kernel(x_ref, g2_ref, wg_ref, wu_ref, wd_ref, gf_ref, out_ref, *, final_norm):
    bf16 = jnp.bfloat16
    x = x_ref[...]
    h = _rmsnorm(x, g2_ref[...]).astype(bf16)
    gate = _dot(h, wg_ref[...])
    up = _dot(h, wu_ref[...])
    hid = (gate * jax.nn.sigmoid(gate) * up).astype(bf16)
    y = x + _dot(hid, wd_ref[...])
    if final_norm:
        y = _rmsnorm(y, gf_ref[...])
    out_ref[...] = y


def _resident(shape):
    return pl.BlockSpec(shape, lambda *_: (0,) * len(shape), pipeline_mode=pl.Buffered(1))


def _nbytes(*arrays):
    return sum(a.size * a.dtype.itemsize for a in arrays)


def _mixer(x, g1, wa, wfz, wb, wfg2, bfg, gn, woa, cw, cb, wob, wo):
    b_sz, s_len, _ = x.shape
    rows = MIX_ROWS
    assert s_len % rows == 0 and rows % CUM_ROWS == 0 and CUM_ROWS % CHUNK == 0
    weights = (g1, wa, wfz, wb, wfg2, bfg, gn, woa, cw, cb, wob, wo)
    tile = pl.BlockSpec((1, rows, D_MODEL), lambda b, s: (b, s, 0))
    scratch = [
        pltpu.VMEM((GLA_HEADS, HEAD_V, HEAD_K), jnp.float32),
        pltpu.VMEM((rows, KEY_WIDTH), jnp.bfloat16),
        pltpu.VMEM((rows, KEY_WIDTH), jnp.bfloat16),
        pltpu.VMEM((rows, VAL_WIDTH), jnp.bfloat16),
        pltpu.VMEM((rows, KEY_WIDTH), jnp.float32),
        pltpu.VMEM((rows, VAL_WIDTH), jnp.float32),
        pltpu.VMEM((rows + V7X_SUBLANES, CONV_CH), jnp.float32),
    ]
    scratch_bytes = (GLA_HEADS * HEAD_V * HEAD_K * 4 + 2 * rows * KEY_WIDTH * 2
                     + rows * VAL_WIDTH * 2 + rows * KEY_WIDTH * 4 + rows * VAL_WIDTH * 4
                     + (rows + V7X_SUBLANES) * CONV_CH * 4)
    return pl.pallas_call(
        _mixer_kernel,
        grid=(b_sz, s_len // rows),
        in_specs=[tile] + [_resident(w.shape) for w in weights],
        out_specs=tile,
        out_shape=jax.ShapeDtypeStruct(x.shape, x.dtype),
        scratch_shapes=scratch,
        compiler_params=pltpu.CompilerParams(
            dimension_semantics=("arbitrary", "arbitrary"),
            vmem_limit_bytes=_vmem_limit(_nbytes(*weights) + scratch_bytes,
                                         2 * rows * D_MODEL * 4)),
        name="mixer",
    )(x, *weights)


def _ffn(x2d, g2, wg, wu, wd, gf, final_norm):
    n_tok, _ = x2d.shape
    rows = FFN_ROWS
    assert n_tok % rows == 0
    weights = (g2, wg, wu, wd, gf)
    tile = pl.BlockSpec((rows, D_MODEL), lambda i: (i, 0))
    return pl.pallas_call(
        functools.partial(_ffn_kernel, final_norm=final_norm),
        grid=(n_tok // rows,),
        in_specs=[tile] + [_resident(w.shape) for w in weights],
        out_specs=tile,
        out_shape=jax.ShapeDtypeStruct(x2d.shape, x2d.dtype),
        compiler_params=pltpu.CompilerParams(
            dimension_semantics=("arbitrary",),
            vmem_limit_bytes=_vmem_limit(_nbytes(*weights), 2 * rows * D_MODEL * 4)),
        name="ffn_final" if final_norm else "ffn",
    )(x2d, *weights)


def kernel(x, norm1_g, w_in, w_fg2, b_fg, gla_norm_g, w_oa, conv_w, conv_b, w_ob, w_o,
           norm2_g, w_ffn_gate, w_ffn_up, w_ffn_down, final_g):
    b_sz, s_len, d = x.shape
    depth = w_in.shape[0]
    bf16 = jnp.bfloat16
    fz0 = _A_WIDTH
    fz1 = _A_WIDTH + GATE_RANK
    for l in range(depth):
        wa = w_in[l, :, :fz0].astype(bf16)
        wfz = jnp.pad(w_in[l, :, fz0:fz1], ((0, 0), (0, _FZ_PAD - GATE_RANK))).astype(bf16)
        wb = w_in[l, :, fz1:].astype(bf16)
        wfg2 = jnp.pad(w_fg2[l], ((0, _FZ_PAD - GATE_RANK), (0, 0))).astype(bf16)
        x = _mixer(x, norm1_g[l][None], wa, wfz, wb, wfg2, b_fg[l][None],
                   gla_norm_g[l][None], w_oa[l].astype(bf16), conv_w[l], conv_b[l][None],
                   w_ob[l].astype(bf16), w_o[l].astype(bf16))
        x = _ffn(x.reshape(b_sz * s_len, d), norm2_g[l][None],
                 w_ffn_gate[l].astype(bf16), w_ffn_up[l].astype(bf16),
                 w_ffn_down[l].astype(bf16), final_g[None],
                 final_norm=(l == depth - 1)).reshape(b_sz, s_len, d)
    return x
```

```python
import functools

import jax
import jax.numpy as jnp
from jax import lax
from jax.experimental import pallas as pl
from jax.experimental.pallas import tpu as pltpu

D_MODEL = 1024
CHUNK = 64
GLA_HEADS = 4
KEY_WIDTH = 512
VAL_WIDTH = 1024
HEAD_K = KEY_WIDTH // GLA_HEADS
HEAD_V = VAL_WIDTH // GLA_HEADS
GATE_RANK = 16
GATE_TAU = 16.0
CONV_CH = 1024
CONV_WIDTH = 3
NORM_EPS = 1e-6

V7X_LANES = 128
V7X_SUBLANES = 8
V7X_MXU_DIM = 256
V7X_VMEM_BYTES = 64 * 1024 * 1024

_A_WIDTH = 2 * KEY_WIDTH + 2 * VAL_WIDTH
_B_WIDTH = 3 * CONV_CH + 2 * D_MODEL
_FZ_PAD = V7X_LANES

MIX_ROWS = 512
FFN_ROWS = 512
CUM_ROWS = V7X_MXU_DIM


def _vmem_limit(resident_bytes, tile_bytes):
    want = resident_bytes + 2 * tile_bytes + 24 * 1024 * 1024
    return int(min(want, V7X_VMEM_BYTES - 6 * 1024 * 1024))


def _rmsnorm(x, g):
    ms = jnp.mean(x * x, axis=-1, keepdims=True)
    return x * lax.rsqrt(ms + NORM_EPS) * g


def _log_sigmoid(x):
    return jnp.minimum(x, 0.0) - jnp.log1p(jnp.exp(-jnp.abs(x)))


def _dot(a, b):
    return jnp.dot(a, b, preferred_element_type=jnp.float32)


def _mixer_kernel(x_ref, g1_ref, wa_ref, wfz_ref, wb_ref, wfg2_ref, bfg_ref,
                  gn_ref, woa_ref, cw_ref, cb_ref, wob_ref, wo_ref,
                  out_ref,
                  state_ref, q_ref, kd_ref, v_ref, lsum_ref, o_ref, u_ref, *, layer):
    rows = x_ref.shape[1]
    bf16 = jnp.bfloat16
    row = slice(layer, layer + 1)

    @pl.when(pl.program_id(1) == 0)
    def _():
        state_ref[...] = jnp.zeros_like(state_ref)
        u_ref[pl.ds(0, V7X_SUBLANES), :] = jnp.zeros((V7X_SUBLANES, CONV_CH), jnp.float32)

    x = x_ref[0]
    h = _rmsnorm(x, g1_ref[row]).astype(bf16)

    n_chunks = rows // CHUNK
    chunk_rows = [slice(c * CHUNK, (c + 1) * CHUNK) for c in range(n_chunks)]
    k_cols = [slice(hd * HEAD_K, (hd + 1) * HEAD_K) for hd in range(GLA_HEADS)]
    v_cols = [slice(hd * HEAD_V, (hd + 1) * HEAD_V) for hd in range(GLA_HEADS)]
    col_b, col_c, col_x, col_ga, col_gb = (
        slice(i * CONV_CH, (i + 1) * CONV_CH) for i in range(5))

    fz = _dot(h, wfz_ref[...]).astype(bf16)
    q_ref[...] = (_dot(h, wa_ref[:, 0:KEY_WIDTH]) * (HEAD_K ** -0.5)).astype(bf16)
    fg = _dot(fz, wfg2_ref[...]) + bfg_ref[row]
    k = _dot(h, wa_ref[:, KEY_WIDTH:2 * KEY_WIDTH])
    v_ref[...] = _dot(h, wa_ref[:, 2 * KEY_WIDTH:2 * KEY_WIDTH + VAL_WIDTH]).astype(bf16)
    log_a = _log_sigmoid(fg) * (1.0 / GATE_TAU)
    la_hi = log_a.astype(bf16)
    la_lo = (log_a - la_hi.astype(jnp.float32)).astype(bf16)

    u = _dot(h, wb_ref[:, col_c]) * _dot(h, wb_ref[:, col_x])
    u_ref[pl.ds(V7X_SUBLANES, rows), :] = u
    u1 = u_ref[pl.ds(V7X_SUBLANES - 1, rows), :]
    u2 = u_ref[pl.ds(V7X_SUBLANES - 2, rows), :]
    cw = cw_ref[layer]
    conv = cw[0:1] * u2 + cw[1:2] * u1 + cw[2:3] * u + cb_ref[row]
    u_ref[pl.ds(0, V7X_SUBLANES), :] = u[rows - V7X_SUBLANES:rows]

    ri = lax.broadcasted_iota(jnp.int32, (CUM_ROWS, CUM_ROWS), 0)
    ci = lax.broadcasted_iota(jnp.int32, (CUM_ROWS, CUM_ROWS), 1)
    later = jnp.where((ri // CHUNK == ci // CHUNK) & (ci > ri), 1.0, 0.0).astype(bf16)
    rev = jnp.concatenate(
        [_dot(later, la_hi[i:i + CUM_ROWS]) + _dot(later, la_lo[i:i + CUM_ROWS])
         for i in range(0, rows, CUM_ROWS)], axis=0)
    yb_in = (_dot(h, wb_ref[:, col_b]) * conv).astype(bf16)
    kd_ref[...] = (k * jnp.exp(rev)).astype(bf16)
    lsum_ref[...] = rev + log_a

    kv = [[lax.dot_general(v_ref[rs, v_cols[hd]], kd_ref[rs, k_cols[hd]],
                           (((0,), (0,)), ((), ())), preferred_element_type=jnp.float32)
           for hd in range(GLA_HEADS)] for rs in chunk_rows]
    y_b = _dot(yb_in, wob_ref[...])
    r = _dot(h, wa_ref[:, 2 * KEY_WIDTH + VAL_WIDTH:_A_WIDTH])

    states = [state_ref[hd] for hd in range(GLA_HEADS)]
    st_bf = []
    for c in range(n_chunks):
        gamma = jnp.exp(lsum_ref[c * CHUNK:c * CHUNK + 1, :])
        for hd in range(GLA_HEADS):
            states[hd] = states[hd] * gamma[:, k_cols[hd]] + kv[c][hd]
        st_bf.append([s.astype(bf16) for s in states])
    for hd in range(GLA_HEADS):
        state_ref[hd] = states[hd]
    gated_b = jax.nn.sigmoid(_dot(h, wb_ref[:, col_gb])) * y_b

    for c, rs in enumerate(chunk_rows):
        for hd in range(GLA_HEADS):
            o_ref[rs, v_cols[hd]] = lax.dot_general(
                q_ref[rs, k_cols[hd]], st_bf[c][hd], (((1,), (1,)), ((), ())),
                preferred_element_type=jnp.float32)
    g_a = jax.nn.sigmoid(_dot(h, wb_ref[:, col_ga]))

    gn = gn_ref[row]
    parts = []
    for hd in range(GLA_HEADS):
        o_h = o_ref[:, v_cols[hd]]
        ms = jnp.mean(o_h * o_h, axis=-1, keepdims=True)
        parts.append(o_h * lax.rsqrt(ms + NORM_EPS) * gn)
    o_n = jnp.concatenate(parts, axis=1)
    y_a = _dot((o_n * (r * jax.nn.sigmoid(r))).astype(bf16), woa_ref[...])

    mix = (g_a * y_a + gated_b).astype(bf16)
    out_ref[0] = x + _dot(mix, wo_ref[...])


def _ffn_kernel(x_ref, g2_ref, wg_ref, wu_ref, wd_ref, gf_ref, out_ref, *, layer, final_norm):
    bf16 = jnp.bfloat16
    x = x_ref[...]
    h = _rmsnorm(x, g2_ref[layer:layer + 1]).astype(bf16)
    gate = _dot(h, wg_ref[...])
    up = _dot(h, wu_ref[...])
    hid = (gate * jax.nn.sigmoid(gate) * up).astype(bf16)
    y = x + _dot(hid, wd_ref[...])
    if final_norm:
        y = _rmsnorm(y, gf_ref[...])
    out_ref[...] = y


def _resident(shape, index=None):
    index = (0,) * len(shape) if index is None else index
    return pl.BlockSpec(shape, lambda *_: index, pipeline_mode=pl.Buffered(1))


def _layer_block(stacked, layer, cols=None, col_block=0):
    _, n_rows, n_cols = stacked.shape
    cols = n_cols if cols is None else cols
    return _resident((None, n_rows, cols), (layer, 0, col_block))


def _block_bytes(spec, dtype):
    n = 1
    for d in spec.block_shape:
        n *= 1 if d is None else d
    return n * jnp.dtype(dtype).itemsize


def _mixer(x, layer, g1, w_in, wb, wfg2, bfg, gn, woa, cw, cb, wob, wo):
    b_sz, s_len, _ = x.shape
    rows = MIX_ROWS
    assert s_len % rows == 0 and rows % CUM_ROWS == 0 and CUM_ROWS % CHUNK == 0
    assert _A_WIDTH % _FZ_PAD == 0
    operands = (g1, w_in, w_in, wb, wfg2, bfg, gn, woa, cw, cb, wob, wo)
    specs = [
        _resident(g1.shape),
        _layer_block(w_in, layer, _A_WIDTH),
        _layer_block(w_in, layer, _FZ_PAD, _A_WIDTH // _FZ_PAD),
        _layer_block(wb, layer),
        _layer_block(wfg2, layer),
        _resident(bfg.shape), _resident(gn.shape),
        _layer_block(woa, layer),
        _resident(cw.shape), _resident(cb.shape),
        _layer_block(wob, layer), _layer_block(wo, layer),
    ]
    resident_bytes = sum(_block_bytes(s, a.dtype) for s, a in zip(specs, operands))
    tile = pl.BlockSpec((1, rows, D_MODEL), lambda b, s: (b, s, 0))
    scratch = [
        pltpu.VMEM((GLA_HEADS, HEAD_V, HEAD_K), jnp.float32),
        pltpu.VMEM((rows, KEY_WIDTH), jnp.bfloat16),
        pltpu.VMEM((rows, KEY_WIDTH), jnp.bfloat16),
        pltpu.VMEM((rows, VAL_WIDTH), jnp.bfloat16),
        pltpu.VMEM((rows, KEY_WIDTH), jnp.float32),
        pltpu.VMEM((rows, VAL_WIDTH), jnp.float32),
        pltpu.VMEM((rows + V7X_SUBLANES, CONV_CH), jnp.float32),
    ]
    scratch_bytes = (GLA_HEADS * HEAD_V * HEAD_K * 4 + 2 * rows * KEY_WIDTH * 2
                     + rows * VAL_WIDTH * 2 + rows * KEY_WIDTH * 4 + rows * VAL_WIDTH * 4
                     + (rows + V7X_SUBLANES) * CONV_CH * 4)
    return pl.pallas_call(
        functools.partial(_mixer_kernel, layer=layer),
        grid=(b_sz, s_len // rows),
        in_specs=[tile] + specs,
        out_specs=tile,
        out_shape=jax.ShapeDtypeStruct(x.shape, x.dtype),
        scratch_shapes=scratch,
        compiler_params=pltpu.CompilerParams(
            dimension_semantics=("arbitrary", "arbitrary"),
            vmem_limit_bytes=_vmem_limit(resident_bytes + scratch_bytes,
                                         2 * rows * D_MODEL * 4)),
        name="mixer",
    )(x, *operands)


def _ffn(x2d, layer, g2, wg, wu, wd, gf, final_norm):
    n_tok, _ = x2d.shape
    rows = FFN_ROWS
    assert n_tok % rows == 0
    operands = (g2, wg, wu, wd, gf)
    specs = [_resident(g2.shape), _layer_block(wg, layer), _layer_block(wu, layer),
             _layer_block(wd, layer), _resident(gf.shape)]
    resident_bytes = sum(_block_bytes(s, a.dtype) for s, a in zip(specs, operands))
    tile = pl.BlockSpec((rows, D_MODEL), lambda i: (i, 0))
    return pl.pallas_call(
        functools.partial(_ffn_kernel, layer=layer, final_norm=final_norm),
        grid=(n_tok // rows,),
        in_specs=[tile] + specs,
        out_specs=tile,
        out_shape=jax.ShapeDtypeStruct(x2d.shape, x2d.dtype),
        compiler_params=pltpu.CompilerParams(
            dimension_semantics=("arbitrary",),
            vmem_limit_bytes=_vmem_limit(resident_bytes, 2 * rows * D_MODEL * 4)),
        name="ffn_final" if final_norm else "ffn",
    )(x2d, *operands)


def kernel(x, norm1_g, w_in, w_fg2, b_fg, gla_norm_g, w_oa, conv_w, conv_b, w_ob, w_o,
           norm2_g, w_ffn_gate, w_ffn_up, w_ffn_down, final_g):
    b_sz, s_len, d = x.shape
    depth = w_in.shape[0]
    bf16 = jnp.bfloat16
    w_in_bf = w_in.astype(bf16)
    wb = w_in_bf[:, :, _A_WIDTH + GATE_RANK:]
    wfg2 = jnp.pad(w_fg2, ((0, 0), (0, _FZ_PAD - GATE_RANK), (0, 0))).astype(bf16)
    woa, wob, wo = w_oa.astype(bf16), w_ob.astype(bf16), w_o.astype(bf16)
    wg, wu, wd = w_ffn_gate.astype(bf16), w_ffn_up.astype(bf16), w_ffn_down.astype(bf16)
    gf = final_g[None]
    for l in range(depth):
        x = _mixer(x, l, norm1_g, w_in_bf, wb, wfg2, b_fg, gla_norm_g, woa,
                   conv_w, conv_b, wob, wo)
        x = _ffn(x.reshape(b_sz * s_len, d), l, norm2_g, wg, wu, wd, gf,
                 final_norm=(l == depth - 1)).reshape(b_sz, s_len, d)
    return x
```

```python
import functools

import jax
import jax.numpy as jnp
from jax import lax
from jax.experimental import pallas as pl
from jax.experimental.pallas import tpu as pltpu

D_MODEL = 1024
CHUNK = 64
GLA_HEADS = 4
KEY_WIDTH = 512
VAL_WIDTH = 1024
HEAD_K = KEY_WIDTH // GLA_HEADS
HEAD_V = VAL_WIDTH // GLA_HEADS
GATE_RANK = 16
GATE_TAU = 16.0
CONV_CH = 1024
CONV_WIDTH = 3
NORM_EPS = 1e-6

V7X_LANES = 128
V7X_SUBLANES = 8
V7X_MXU_DIM = 256
V7X_VMEM_BYTES = 64 * 1024 * 1024

_A_WIDTH = 2 * KEY_WIDTH + 2 * VAL_WIDTH
_B_WIDTH = 3 * CONV_CH + 2 * D_MODEL
_FZ_PAD = V7X_LANES

MIX_ROWS = 512
FFN_ROWS = 1024
FFN_GROUP = 512
CUM_ROWS = V7X_MXU_DIM


def _vmem_limit(resident_bytes, tile_bytes):
    want = resident_bytes + 2 * tile_bytes + 24 * 1024 * 1024
    return int(min(want, V7X_VMEM_BYTES - 6 * 1024 * 1024))


def _rmsnorm(x, g):
    ms = jnp.mean(x * x, axis=-1, keepdims=True)
    return x * lax.rsqrt(ms + NORM_EPS) * g


def _log_sigmoid(x):
    return jnp.minimum(x, 0.0) - jnp.log1p(jnp.exp(-jnp.abs(x)))


def _dot(a, b):
    return jnp.dot(a, b, preferred_element_type=jnp.float32)


def _mixer_kernel(x_ref, g1_ref, wa_ref, wfz_ref, wb_ref, wfg2_ref, bfg_ref,
                  gn_ref, woa_ref, cw_ref, cb_ref, wob_ref, wo_ref,
                  out_ref,
                  state_ref, q_ref, kd_ref, v_ref, lsum_ref, o_ref, u_ref, st_ref, *, layer):
    rows = x_ref.shape[1]
    bf16 = jnp.bfloat16
    row = slice(layer, layer + 1)

    @pl.when(pl.program_id(1) == 0)
    def _():
        state_ref[...] = jnp.zeros_like(state_ref)
        u_ref[pl.ds(0, V7X_SUBLANES), :] = jnp.zeros((V7X_SUBLANES, CONV_CH), jnp.float32)

    x = x_ref[0]
    h = _rmsnorm(x, g1_ref[row]).astype(bf16)

    n_chunks = rows // CHUNK
    chunk_rows = [slice(c * CHUNK, (c + 1) * CHUNK) for c in range(n_chunks)]
    k_cols = [slice(hd * HEAD_K, (hd + 1) * HEAD_K) for hd in range(GLA_HEADS)]
    v_cols = [slice(hd * HEAD_V, (hd + 1) * HEAD_V) for hd in range(GLA_HEADS)]
    col_b, col_c, col_x, col_ga, col_gb = (
        slice(i * CONV_CH, (i + 1) * CONV_CH) for i in range(5))

    fz = _dot(h, wfz_ref[...]).astype(bf16)
    q_ref[...] = (_dot(h, wa_ref[:, 0:KEY_WIDTH]) * (HEAD_K ** -0.5)).astype(bf16)
    fg = _dot(fz, wfg2_ref[...]) + bfg_ref[row]
    k = _dot(h, wa_ref[:, KEY_WIDTH:2 * KEY_WIDTH])
    v_ref[...] = _dot(h, wa_ref[:, 2 * KEY_WIDTH:2 * KEY_WIDTH + VAL_WIDTH]).astype(bf16)
    log_a = _log_sigmoid(fg) * (1.0 / GATE_TAU)
    la_hi = log_a.astype(bf16)
    la_lo = (log_a - la_hi.astype(jnp.float32)).astype(bf16)

    u = _dot(h, wb_ref[:, col_c]) * _dot(h, wb_ref[:, col_x])
    u_ref[pl.ds(V7X_SUBLANES, rows), :] = u
    u1 = u_ref[pl.ds(V7X_SUBLANES - 1, rows), :]
    u2 = u_ref[pl.ds(V7X_SUBLANES - 2, rows), :]
    cw = cw_ref[layer]
    conv = cw[0:1] * u2 + cw[1:2] * u1 + cw[2:3] * u + cb_ref[row]
    u_ref[pl.ds(0, V7X_SUBLANES), :] = u[rows - V7X_SUBLANES:rows]

    ri = lax.broadcasted_iota(jnp.int32, (CUM_ROWS, CUM_ROWS), 0)
    ci = lax.broadcasted_iota(jnp.int32, (CUM_ROWS, CUM_ROWS), 1)
    later = jnp.where((ri // CHUNK == ci // CHUNK) & (ci > ri), 1.0, 0.0).astype(bf16)
    rev = jnp.concatenate(
        [_dot(later, la_hi[i:i + CUM_ROWS]) + _dot(later, la_lo[i:i + CUM_ROWS])
         for i in range(0, rows, CUM_ROWS)], axis=0)
    yb_in = (_dot(h, wb_ref[:, col_b]) * conv).astype(bf16)
    kd_ref[...] = (k * jnp.exp(rev)).astype(bf16)
    lsum_ref[...] = rev + log_a

    kv = [[lax.dot_general(v_ref[rs, v_cols[hd]], kd_ref[rs, k_cols[hd]],
                           (((0,), (0,)), ((), ())), preferred_element_type=jnp.float32)
           for hd in range(GLA_HEADS)] for rs in chunk_rows]
    y_b = _dot(yb_in, wob_ref[...])

    states = [state_ref[hd] for hd in range(GLA_HEADS)]
    for c in range(n_chunks):
        gamma = jnp.exp(lsum_ref[c * CHUNK:c * CHUNK + 1, :])
        for hd in range(GLA_HEADS):
            states[hd] = states[hd] * gamma[:, k_cols[hd]] + kv[c][hd]
        for hd in range(GLA_HEADS):
            st_ref[c * GLA_HEADS + hd] = states[hd].astype(bf16).T
    for hd in range(GLA_HEADS):
        state_ref[hd] = states[hd]
    gated_b = jax.nn.sigmoid(_dot(h, wb_ref[:, col_gb])) * y_b

    r_base = 2 * KEY_WIDTH + VAL_WIDTH
    ga_base = col_ga.start
    fill_w = D_MODEL * 2 // n_chunks
    r_parts, ga_parts = [], []
    for c, rs in enumerate(chunk_rows):
        for hd in range(GLA_HEADS):
            o_ref[rs, v_cols[hd]] = _dot(q_ref[rs, k_cols[hd]], st_ref[c * GLA_HEADS + hd])
        lo = c * fill_w
        if lo < D_MODEL:
            r_parts.append(_dot(h, wa_ref[:, r_base + lo:r_base + lo + fill_w]))
        else:
            lo -= D_MODEL
            ga_parts.append(_dot(h, wb_ref[:, ga_base + lo:ga_base + lo + fill_w]))
    r = jnp.concatenate(r_parts, axis=1)
    g_a = jax.nn.sigmoid(jnp.concatenate(ga_parts, axis=1))

    gn = gn_ref[row]
    parts = []
    for hd in range(GLA_HEADS):
        o_h = o_ref[:, v_cols[hd]]
        ms = jnp.mean(o_h * o_h, axis=-1, keepdims=True)
        parts.append(o_h * lax.rsqrt(ms + NORM_EPS) * gn)
    o_n = jnp.concatenate(parts, axis=1)
    y_a = _dot((o_n * (r * jax.nn.sigmoid(r))).astype(bf16), woa_ref[...])

    mix = (g_a * y_a + gated_b).astype(bf16)
    out_ref[0] = x + _dot(mix, wo_ref[...])


def _ffn_kernel(x_ref, g2_ref, wg_ref, wu_ref, wd_ref, gf_ref, out_ref, *, layer, final_norm):
    bf16 = jnp.bfloat16
    rows = x_ref.shape[0]
    groups = [slice(i, i + FFN_GROUP) for i in range(0, rows, FFN_GROUP)]
    g2 = g2_ref[layer:layer + 1]
    xs = [x_ref[g, :] for g in groups]
    hs = [_rmsnorm(x, g2).astype(bf16) for x in xs]
    hids = []
    for h in hs:
        gate = _dot(h, wg_ref[...])
        up = _dot(h, wu_ref[...])
        hids.append((gate * jax.nn.sigmoid(gate) * up).astype(bf16))
    for g, x, hid in zip(groups, xs, hids):
        y = x + _dot(hid, wd_ref[...])
        if final_norm:
            y = _rmsnorm(y, gf_ref[...])
        out_ref[g, :] = y


def _resident(shape, index=None):
    index = (0,) * len(shape) if index is None else index
    return pl.BlockSpec(shape, lambda *_: index, pipeline_mode=pl.Buffered(1))


def _layer_block(stacked, layer, cols=None, col_block=0):
    _, n_rows, n_cols = stacked.shape
    cols = n_cols if cols is None else cols
    return _resident((None, n_rows, cols), (layer, 0, col_block))


def _block_bytes(spec, dtype):
    n = 1
    for d in spec.block_shape:
        n *= 1 if d is None else d
    return n * jnp.dtype(dtype).itemsize


def _mixer(x, layer, g1, w_in, wb, wfg2, bfg, gn, woa, cw, cb, wob, wo):
    b_sz, s_len, _ = x.shape
    rows = MIX_ROWS
    assert s_len % rows == 0 and rows % CUM_ROWS == 0 and CUM_ROWS % CHUNK == 0
    assert _A_WIDTH % _FZ_PAD == 0
    operands = (g1, w_in, w_in, wb, wfg2, bfg, gn, woa, cw, cb, wob, wo)
    specs = [
        _resident(g1.shape),
        _layer_block(w_in, layer, _A_WIDTH),
        _layer_block(w_in, layer, _FZ_PAD, _A_WIDTH // _FZ_PAD),
        _layer_block(wb, layer),
        _layer_block(wfg2, layer),
        _resident(bfg.shape), _resident(gn.shape),
        _layer_block(woa, layer),
        _resident(cw.shape), _resident(cb.shape),
        _layer_block(wob, layer), _layer_block(wo, layer),
    ]
    resident_bytes = sum(_block_bytes(s, a.dtype) for s, a in zip(specs, operands))
    tile = pl.BlockSpec((1, rows, D_MODEL), lambda b, s: (b, s, 0))
    scratch = [
        pltpu.VMEM((GLA_HEADS, HEAD_V, HEAD_K), jnp.float32),
        pltpu.VMEM((rows, KEY_WIDTH), jnp.bfloat16),
        pltpu.VMEM((rows, KEY_WIDTH), jnp.bfloat16),
        pltpu.VMEM((rows, VAL_WIDTH), jnp.bfloat16),
        pltpu.VMEM((rows, KEY_WIDTH), jnp.float32),
        pltpu.VMEM((rows, VAL_WIDTH), jnp.float32),
        pltpu.VMEM((rows + V7X_SUBLANES, CONV_CH), jnp.float32),
        pltpu.VMEM((rows // CHUNK * GLA_HEADS, HEAD_K, HEAD_V), jnp.bfloat16),
    ]
    scratch_bytes = (GLA_HEADS * HEAD_V * HEAD_K * 4 + 2 * rows * KEY_WIDTH * 2
                     + rows * VAL_WIDTH * 2 + rows * KEY_WIDTH * 4 + rows * VAL_WIDTH * 4
                     + (rows + V7X_SUBLANES) * CONV_CH * 4
                     + rows // CHUNK * GLA_HEADS * HEAD_K * HEAD_V * 2)
    return pl.pallas_call(
        functools.partial(_mixer_kernel, layer=layer),
        grid=(b_sz, s_len // rows),
        in_specs=[tile] + specs,
        out_specs=tile,
        out_shape=jax.ShapeDtypeStruct(x.shape, x.dtype),
        scratch_shapes=scratch,
        compiler_params=pltpu.CompilerParams(
            dimension_semantics=("arbitrary", "arbitrary"),
            vmem_limit_bytes=_vmem_limit(resident_bytes + scratch_bytes,
                                         2 * rows * D_MODEL * 4)),
        name="mixer",
    )(x, *operands)


def _ffn(x2d, layer, g2, wg, wu, wd, gf, final_norm):
    n_tok, _ = x2d.shape
    rows = FFN_ROWS
    assert n_tok % rows == 0
    operands = (g2, wg, wu, wd, gf)
    specs = [_resident(g2.shape), _layer_block(wg, layer), _layer_block(wu, layer),
             _layer_block(wd, layer), _resident(gf.shape)]
    resident_bytes = sum(_block_bytes(s, a.dtype) for s, a in zip(specs, operands))
    tile = pl.BlockSpec((rows, D_MODEL), lambda i: (i, 0))
    return pl.pallas_call(
        functools.partial(_ffn_kernel, layer=layer, final_norm=final_norm),
        grid=(n_tok // rows,),
        in_specs=[tile] + specs,
        out_specs=tile,
        out_shape=jax.ShapeDtypeStruct(x2d.shape, x2d.dtype),
        compiler_params=pltpu.CompilerParams(
            dimension_semantics=("arbitrary",),
            vmem_limit_bytes=_vmem_limit(resident_bytes, 2 * rows * D_MODEL * 4)),
        name="ffn_final" if final_norm else "ffn",
    )(x2d, *operands)


def kernel(x, norm1_g, w_in, w_fg2, b_fg, gla_norm_g, w_oa, conv_w, conv_b, w_ob, w_o,
           norm2_g, w_ffn_gate, w_ffn_up, w_ffn_down, final_g):
    b_sz, s_len, d = x.shape
    depth = w_in.shape[0]
    bf16 = jnp.bfloat16
    w_in_bf = w_in.astype(bf16)
    wb = w_in_bf[:, :, _A_WIDTH + GATE_RANK:]
    wfg2 = jnp.pad(w_fg2, ((0, 0), (0, _FZ_PAD - GATE_RANK), (0, 0))).astype(bf16)
    woa, wob, wo = w_oa.astype(bf16), w_ob.astype(bf16), w_o.astype(bf16)
    wg, wu, wd = w_ffn_gate.astype(bf16), w_ffn_up.astype(bf16), w_ffn_down.astype(bf16)
    gf = final_g[None]
    for l in range(depth):
        x = _mixer(x, l, norm1_g, w_in_bf, wb, wfg2, b_fg, gla_norm_g, woa,
                   conv_w, conv_b, wob, wo)
        x = _ffn(x.reshape(b_sz * s_len, d), l, norm2_g, wg, wu, wd, gf,
                 final_norm=(l == depth - 1)).reshape(b_sz, s_len, d)
    return x
```

```python
import functools

import jax
import jax.numpy as jnp
from jax import lax
from jax.experimental import pallas as pl
from jax.experimental.pallas import tpu as pltpu

D_MODEL = 1024
CHUNK = 64
GLA_HEADS = 4
KEY_WIDTH = 512
VAL_WIDTH = 1024
HEAD_K = KEY_WIDTH // GLA_HEADS
HEAD_V = VAL_WIDTH // GLA_HEADS
GATE_RANK = 16
GATE_TAU = 16.0
CONV_CH = 1024
CONV_WIDTH = 3
NORM_EPS = 1e-6

V7X_LANES = 128
V7X_SUBLANES = 8
V7X_MXU_DIM = 256
V7X_VMEM_BYTES = 64 * 1024 * 1024

_A_WIDTH = 2 * KEY_WIDTH + 2 * VAL_WIDTH
_FZ_PAD = V7X_LANES

MIX_ROWS = 512
FFN_ROWS = 1024
FFN_GROUP = 512
CUM_ROWS = V7X_MXU_DIM


def _vmem_limit(resident_bytes, tile_bytes):
    want = resident_bytes + 2 * tile_bytes + 24 * 1024 * 1024
    return int(min(want, V7X_VMEM_BYTES - 6 * 1024 * 1024))


def _rmsnorm(x, g):
    ms = jnp.mean(x * x, axis=-1, keepdims=True)
    return x * lax.rsqrt(ms + NORM_EPS) * g


def _log_sigmoid(x):
    return jnp.minimum(x, 0.0) - jnp.log1p(jnp.exp(-jnp.abs(x)))


def _dot(a, b):
    return jnp.dot(a, b, preferred_element_type=jnp.float32)


def _mixer_kernel(x_ref, g1_ref, win_ref, wfg2_ref, bfg_ref,
                  gn_ref, woa_ref, cw_ref, cb_ref, wob_ref, wo_ref,
                  out_ref,
                  state_ref, q_ref, kd_ref, v_ref, lsum_ref, o_ref, u_ref, st_ref, *, layer):
    rows = x_ref.shape[1]
    bf16 = jnp.bfloat16
    row = slice(layer, layer + 1)

    @pl.when(pl.program_id(1) == 0)
    def _():
        state_ref[...] = jnp.zeros_like(state_ref)
        u_ref[pl.ds(0, V7X_SUBLANES), :] = jnp.zeros((V7X_SUBLANES, CONV_CH), jnp.float32)

    x = x_ref[0]
    h = _rmsnorm(x, g1_ref[row]).astype(bf16)

    def proj(lo, width):
        return lax.dot_general(h, win_ref[lo:lo + width, :], (((1,), (1,)), ((), ())),
                               preferred_element_type=jnp.float32)

    n_chunks = rows // CHUNK
    chunk_rows = [slice(c * CHUNK, (c + 1) * CHUNK) for c in range(n_chunks)]
    k_cols = [slice(hd * HEAD_K, (hd + 1) * HEAD_K) for hd in range(GLA_HEADS)]
    v_cols = [slice(hd * HEAD_V, (hd + 1) * HEAD_V) for hd in range(GLA_HEADS)]
    col_q, col_k, col_v, col_r = 0, KEY_WIDTH, 2 * KEY_WIDTH, 2 * KEY_WIDTH + VAL_WIDTH
    col_fz = _A_WIDTH
    col_b, col_c, col_x, col_ga, col_gb = (
        _A_WIDTH + GATE_RANK + i * CONV_CH for i in range(5))

    fz = proj(col_fz, _FZ_PAD).astype(bf16)
    q_ref[...] = (proj(col_q, KEY_WIDTH) * (HEAD_K ** -0.5)).astype(bf16)
    fg = _dot(fz, wfg2_ref[...]) + bfg_ref[row]
    k = proj(col_k, KEY_WIDTH)
    v_ref[...] = proj(col_v, VAL_WIDTH).astype(bf16)
    log_a = _log_sigmoid(fg) * (1.0 / GATE_TAU)
    la_hi = log_a.astype(bf16)
    la_lo = (log_a - la_hi.astype(jnp.float32)).astype(bf16)

    u = proj(col_c, CONV_CH) * proj(col_x, CONV_CH)
    u_ref[pl.ds(V7X_SUBLANES, rows), :] = u
    u1 = u_ref[pl.ds(V7X_SUBLANES - 1, rows), :]
    u2 = u_ref[pl.ds(V7X_SUBLANES - 2, rows), :]
    cw = cw_ref[layer]
    conv = cw[0:1] * u2 + cw[1:2] * u1 + cw[2:3] * u + cb_ref[row]
    u_ref[pl.ds(0, V7X_SUBLANES), :] = u[rows - V7X_SUBLANES:rows]

    ri = lax.broadcasted_iota(jnp.int32, (CUM_ROWS, CUM_ROWS), 0)
    ci = lax.broadcasted_iota(jnp.int32, (CUM_ROWS, CUM_ROWS), 1)
    later = jnp.where((ri // CHUNK == ci // CHUNK) & (ci > ri), 1.0, 0.0).astype(bf16)
    rev = jnp.concatenate(
        [_dot(later, la_hi[i:i + CUM_ROWS]) + _dot(later, la_lo[i:i + CUM_ROWS])
         for i in range(0, rows, CUM_ROWS)], axis=0)
    yb_in = (proj(col_b, CONV_CH) * conv).astype(bf16)
    kd_ref[...] = (k * jnp.exp(rev)).astype(bf16)
    lsum_ref[...] = rev + log_a

    kv = [[lax.dot_general(v_ref[rs, v_cols[hd]], kd_ref[rs, k_cols[hd]],
                           (((0,), (0,)), ((), ())), preferred_element_type=jnp.float32)
           for hd in range(GLA_HEADS)] for rs in chunk_rows]
    y_b = _dot(yb_in, wob_ref[...])

    states = [state_ref[hd] for hd in range(GLA_HEADS)]
    for c in range(n_chunks):
        gamma = jnp.exp(lsum_ref[c * CHUNK:c * CHUNK + 1, :])
        for hd in range(GLA_HEADS):
            states[hd] = states[hd] * gamma[:, k_cols[hd]] + kv[c][hd]
        for hd in range(GLA_HEADS):
            st_ref[c * GLA_HEADS + hd] = states[hd].astype(bf16).T
    for hd in range(GLA_HEADS):
        state_ref[hd] = states[hd]
    gated_b = jax.nn.sigmoid(proj(col_gb, D_MODEL)) * y_b

    fill_w = D_MODEL * 2 // n_chunks
    r_parts, ga_parts = [], []
    for c, rs in enumerate(chunk_rows):
        for hd in range(GLA_HEADS):
            o_ref[rs, v_cols[hd]] = _dot(q_ref[rs, k_cols[hd]], st_ref[c * GLA_HEADS + hd])
        lo = c * fill_w
        if lo < D_MODEL:
            r_parts.append(proj(col_r + lo, fill_w))
        else:
            lo -= D_MODEL
            ga_parts.append(proj(col_ga + lo, fill_w))
    r = jnp.concatenate(r_parts, axis=1)
    g_a = jax.nn.sigmoid(jnp.concatenate(ga_parts, axis=1))

    gn = gn_ref[row]
    parts = []
    for hd in range(GLA_HEADS):
        o_h = o_ref[:, v_cols[hd]]
        ms = jnp.mean(o_h * o_h, axis=-1, keepdims=True)
        parts.append(o_h * lax.rsqrt(ms + NORM_EPS) * gn)
    o_n = jnp.concatenate(parts, axis=1)
    y_a = _dot((o_n * (r * jax.nn.sigmoid(r))).astype(bf16), woa_ref[...])

    mix = (g_a * y_a + gated_b).astype(bf16)
    out_ref[0] = x + _dot(mix, wo_ref[...])


def _ffn_kernel(x_ref, g2_ref, wg_ref, wu_ref, wd_ref, gf_ref, out_ref, *, layer, final_norm):
    bf16 = jnp.bfloat16
    rows = x_ref.shape[0]
    groups = [slice(i, i + FFN_GROUP) for i in range(0, rows, FFN_GROUP)]
    g2 = g2_ref[layer:layer + 1]
    xs = [x_ref[g, :] for g in groups]
    hs = [_rmsnorm(x, g2).astype(bf16) for x in xs]
    hids = []
    for h in hs:
        gate = _dot(h, wg_ref[...])
        up = _dot(h, wu_ref[...])
        hids.append((gate * jax.nn.sigmoid(gate) * up).astype(bf16))
    for g, x, hid in zip(groups, xs, hids):
        y = x + _dot(hid, wd_ref[...])
        if final_norm:
            y = _rmsnorm(y, gf_ref[...])
        out_ref[g, :] = y


def _resident(shape, index=None):
    index = (0,) * len(shape) if index is None else index
    return pl.BlockSpec(shape, lambda *_: index, pipeline_mode=pl.Buffered(1))


def _layer_block(stacked, layer):
    _, n_rows, n_cols = stacked.shape
    return _resident((None, n_rows, n_cols), (layer, 0, 0))


def _block_bytes(spec, dtype):
    n = 1
    for d in spec.block_shape:
        n *= 1 if d is None else d
    return n * jnp.dtype(dtype).itemsize


def _mixer(x, layer, g1, w_in_t, wfg2, bfg, gn, woa, cw, cb, wob, wo):
    b_sz, s_len, _ = x.shape
    rows = MIX_ROWS
    assert s_len % rows == 0 and rows % CUM_ROWS == 0 and CUM_ROWS % CHUNK == 0
    operands = (g1, w_in_t, wfg2, bfg, gn, woa, cw, cb, wob, wo)
    specs = [
        _resident(g1.shape),
        _layer_block(w_in_t, layer),
        _layer_block(wfg2, layer),
        _resident(bfg.shape), _resident(gn.shape),
        _layer_block(woa, layer),
        _resident(cw.shape), _resident(cb.shape),
        _layer_block(wob, layer), _layer_block(wo, layer),
    ]
    resident_bytes = sum(_block_bytes(s, a.dtype) for s, a in zip(specs, operands))
    tile = pl.BlockSpec((1, rows, D_MODEL), lambda b, s: (b, s, 0))
    scratch = [
        pltpu.VMEM((GLA_HEADS, HEAD_V, HEAD_K), jnp.float32),
        pltpu.VMEM((rows, KEY_WIDTH), jnp.bfloat16),
        pltpu.VMEM((rows, KEY_WIDTH), jnp.bfloat16),
        pltpu.VMEM((rows, VAL_WIDTH), jnp.bfloat16),
        pltpu.VMEM((rows, KEY_WIDTH), jnp.float32),
        pltpu.VMEM((rows, VAL_WIDTH), jnp.float32),
        pltpu.VMEM((rows + V7X_SUBLANES, CONV_CH), jnp.float32),
        pltpu.VMEM((rows // CHUNK * GLA_HEADS, HEAD_K, HEAD_V), jnp.bfloat16),
    ]
    scratch_bytes = (GLA_HEADS * HEAD_V * HEAD_K * 4 + 2 * rows * KEY_WIDTH * 2
                     + rows * VAL_WIDTH * 2 + rows * KEY_WIDTH * 4 + rows * VAL_WIDTH * 4
                     + (rows + V7X_SUBLANES) * CONV_CH * 4
                     + rows // CHUNK * GLA_HEADS * HEAD_K * HEAD_V * 2)
    return pl.pallas_call(
        functools.partial(_mixer_kernel, layer=layer),
        grid=(b_sz, s_len // rows),
        in_specs=[tile] + specs,
        out_specs=tile,
        out_shape=jax.ShapeDtypeStruct(x.shape, x.dtype),
        scratch_shapes=scratch,
        compiler_params=pltpu.CompilerParams(
            dimension_semantics=("arbitrary", "arbitrary"),
            vmem_limit_bytes=_vmem_limit(resident_bytes + scratch_bytes,
                                         2 * rows * D_MODEL * 4)),
        name="mixer",
    )(x, *operands)


def _ffn(x2d, layer, g2, wg, wu, wd, gf, final_norm):
    n_tok, _ = x2d.shape
    rows = FFN_ROWS
    assert n_tok % rows == 0
    operands = (g2, wg, wu, wd, gf)
    specs = [_resident(g2.shape), _layer_block(wg, layer), _layer_block(wu, layer),
             _layer_block(wd, layer), _resident(gf.shape)]
    resident_bytes = sum(_block_bytes(s, a.dtype) for s, a in zip(specs, operands))
    tile = pl.BlockSpec((rows, D_MODEL), lambda i: (i, 0))
    return pl.pallas_call(
        functools.partial(_ffn_kernel, layer=layer, final_norm=final_norm),
        grid=(n_tok // rows,),
        in_specs=[tile] + specs,
        out_specs=tile,
        out_shape=jax.ShapeDtypeStruct(x2d.shape, x2d.dtype),
        compiler_params=pltpu.CompilerParams(
            dimension_semantics=("arbitrary",),
            vmem_limit_bytes=_vmem_limit(resident_bytes, 2 * rows * D_MODEL * 4)),
        name="ffn_final" if final_norm else "ffn",
    )(x2d, *operands)


def kernel(x, norm1_g, w_in, w_fg2, b_fg, gla_norm_g, w_oa, conv_w, conv_b, w_ob, w_o,
           norm2_g, w_ffn_gate, w_ffn_up, w_ffn_down, final_g):
    b_sz, s_len, d = x.shape
    depth = w_in.shape[0]
    bf16 = jnp.bfloat16
    w_in_t = jnp.swapaxes(w_in, 1, 2).astype(bf16)
    wfg2 = jnp.pad(w_fg2, ((0, 0), (0, _FZ_PAD - GATE_RANK), (0, 0))).astype(bf16)
    woa, wob, wo = w_oa.astype(bf16), w_ob.astype(bf16), w_o.astype(bf16)
    wg, wu, wd = w_ffn_gate.astype(bf16), w_ffn_up.astype(bf16), w_ffn_down.astype(bf16)
    gf = final_g[None]
    for l in range(depth):
        x = _mixer(x, l, norm1_g, w_in_t, wfg2, b_fg, gla_norm_g, woa,
                   conv_w, conv_b, wob, wo)
        x = _ffn(x.reshape(b_sz * s_len, d), l, norm2_g, wg, wu, wd, gf,
                 final_norm=(l == depth - 1)).reshape(b_sz, s_len, d)
    return x
```

```python
import functools

import jax
import jax.numpy as jnp
from jax import lax
from jax.experimental import pallas as pl
from jax.experimental.pallas import tpu as pltpu

D_MODEL = 1024
CHUNK = 64
GLA_HEADS = 4
KEY_WIDTH = 512
VAL_WIDTH = 1024
HEAD_K = KEY_WIDTH // GLA_HEADS
HEAD_V = VAL_WIDTH // GLA_HEADS
GATE_RANK = 16
GATE_TAU = 16.0
CONV_CH = 1024
CONV_WIDTH = 3
NORM_EPS = 1e-6

V7X_LANES = 128
V7X_SUBLANES = 8
V7X_MXU_DIM = 256
V7X_VMEM_BYTES = 64 * 1024 * 1024

_A_WIDTH = 2 * KEY_WIDTH + 2 * VAL_WIDTH
_B_WIDTH = 3 * CONV_CH + 2 * D_MODEL
_FZ_PAD = V7X_LANES

MIX_ROWS = 512
FFN_ROWS = 1024
FFN_GROUP = 512
CUM_ROWS = V7X_MXU_DIM
PREP_COLS_A = 640
PREP_COLS_B = 1024


def _vmem_limit(resident_bytes, tile_bytes):
    want = resident_bytes + 2 * tile_bytes + 24 * 1024 * 1024
    return int(min(want, V7X_VMEM_BYTES - 6 * 1024 * 1024))


def _rmsnorm(x, g):
    ms = jnp.mean(x * x, axis=-1, keepdims=True)
    return x * lax.rsqrt(ms + NORM_EPS) * g


def _log_sigmoid(x):
    return jnp.minimum(x, 0.0) - jnp.log1p(jnp.exp(-jnp.abs(x)))


def _dot(a, b):
    return jnp.dot(a, b, preferred_element_type=jnp.float32)


def _mixer_kernel(x_ref, g1_ref, wa_ref, wfz_ref, wb_ref, wfg2_ref, bfg_ref,
                  gn_ref, woa_ref, cw_ref, cb_ref, wob_ref, wo_ref,
                  out_ref,
                  state_ref, q_ref, kd_ref, v_ref, lsum_ref, o_ref, u_ref, st_ref, *, layer):
    rows = x_ref.shape[1]
    bf16 = jnp.bfloat16
    row = slice(layer, layer + 1)

    @pl.when(pl.program_id(1) == 0)
    def _():
        state_ref[...] = jnp.zeros_like(state_ref)
        u_ref[pl.ds(0, V7X_SUBLANES), :] = jnp.zeros((V7X_SUBLANES, CONV_CH), jnp.float32)

    x = x_ref[0]
    h = _rmsnorm(x, g1_ref[row]).astype(bf16)

    n_chunks = rows // CHUNK
    chunk_rows = [slice(c * CHUNK, (c + 1) * CHUNK) for c in range(n_chunks)]
    k_cols = [slice(hd * HEAD_K, (hd + 1) * HEAD_K) for hd in range(GLA_HEADS)]
    v_cols = [slice(hd * HEAD_V, (hd + 1) * HEAD_V) for hd in range(GLA_HEADS)]
    col_b, col_c, col_x, col_ga, col_gb = (
        slice(i * CONV_CH, (i + 1) * CONV_CH) for i in range(5))

    fz = _dot(h, wfz_ref[...]).astype(bf16)
    q_ref[...] = (_dot(h, wa_ref[:, 0:KEY_WIDTH]) * (HEAD_K ** -0.5)).astype(bf16)
    fg = _dot(fz, wfg2_ref[...]) + bfg_ref[row]
    k = _dot(h, wa_ref[:, KEY_WIDTH:2 * KEY_WIDTH])
    v_ref[...] = _dot(h, wa_ref[:, 2 * KEY_WIDTH:2 * KEY_WIDTH + VAL_WIDTH]).astype(bf16)
    log_a = _log_sigmoid(fg) * (1.0 / GATE_TAU)
    la_hi = log_a.astype(bf16)
    la_lo = (log_a - la_hi.astype(jnp.float32)).astype(bf16)

    u = _dot(h, wb_ref[:, col_c]) * _dot(h, wb_ref[:, col_x])
    u_ref[pl.ds(V7X_SUBLANES, rows), :] = u
    u1 = u_ref[pl.ds(V7X_SUBLANES - 1, rows), :]
    u2 = u_ref[pl.ds(V7X_SUBLANES - 2, rows), :]
    cw = cw_ref[layer]
    conv = cw[0:1] * u2 + cw[1:2] * u1 + cw[2:3] * u + cb_ref[row]
    u_ref[pl.ds(0, V7X_SUBLANES), :] = u[rows - V7X_SUBLANES:rows]

    ri = lax.broadcasted_iota(jnp.int32, (CUM_ROWS, CUM_ROWS), 0)
    ci = lax.broadcasted_iota(jnp.int32, (CUM_ROWS, CUM_ROWS), 1)
    later = jnp.where((ri // CHUNK == ci // CHUNK) & (ci > ri), 1.0, 0.0).astype(bf16)
    rev = jnp.concatenate(
        [_dot(later, la_hi[i:i + CUM_ROWS]) + _dot(later, la_lo[i:i + CUM_ROWS])
         for i in range(0, rows, CUM_ROWS)], axis=0)
    yb_in = (_dot(h, wb_ref[:, col_b]) * conv).astype(bf16)
    kd_ref[...] = (k * jnp.exp(rev)).astype(bf16)
    lsum_ref[...] = rev + log_a

    kv = [[lax.dot_general(v_ref[rs, v_cols[hd]], kd_ref[rs, k_cols[hd]],
                           (((0,), (0,)), ((), ())), preferred_element_type=jnp.float32)
           for hd in range(GLA_HEADS)] for rs in chunk_rows]
    y_b = _dot(yb_in, wob_ref[...])

    states = [state_ref[hd] for hd in range(GLA_HEADS)]
    for c in range(n_chunks):
        gamma = jnp.exp(lsum_ref[c * CHUNK:c * CHUNK + 1, :])
        for hd in range(GLA_HEADS):
            states[hd] = states[hd] * gamma[:, k_cols[hd]] + kv[c][hd]
        for hd in range(GLA_HEADS):
            st_ref[c * GLA_HEADS + hd] = states[hd].astype(bf16).T
    for hd in range(GLA_HEADS):
        state_ref[hd] = states[hd]
    gated_b = jax.nn.sigmoid(_dot(h, wb_ref[:, col_gb])) * y_b

    r_base = 2 * KEY_WIDTH + VAL_WIDTH
    ga_base = col_ga.start
    fill_w = D_MODEL * 2 // n_chunks
    r_parts, ga_parts = [], []
    for c, rs in enumerate(chunk_rows):
        for hd in range(GLA_HEADS):
            o_ref[rs, v_cols[hd]] = _dot(q_ref[rs, k_cols[hd]], st_ref[c * GLA_HEADS + hd])
        lo = c * fill_w
        if lo < D_MODEL:
            r_parts.append(_dot(h, wa_ref[:, r_base + lo:r_base + lo + fill_w]))
        else:
            lo -= D_MODEL
            ga_parts.append(_dot(h, wb_ref[:, ga_base + lo:ga_base + lo + fill_w]))
    r = jnp.concatenate(r_parts, axis=1)
    g_a = jax.nn.sigmoid(jnp.concatenate(ga_parts, axis=1))

    gn = gn_ref[row]
    parts = []
    for hd in range(GLA_HEADS):
        o_h = o_ref[:, v_cols[hd]]
        ms = jnp.mean(o_h * o_h, axis=-1, keepdims=True)
        parts.append(o_h * lax.rsqrt(ms + NORM_EPS) * gn)
    o_n = jnp.concatenate(parts, axis=1)
    y_a = _dot((o_n * (r * jax.nn.sigmoid(r))).astype(bf16), woa_ref[...])

    mix = (g_a * y_a + gated_b).astype(bf16)
    out_ref[0] = x + _dot(mix, wo_ref[...])


def _ffn_kernel(x_ref, g2_ref, wg_ref, wu_ref, wd_ref, gf_ref, out_ref, *, layer, final_norm):
    bf16 = jnp.bfloat16
    rows = x_ref.shape[0]
    groups = [slice(i, i + FFN_GROUP) for i in range(0, rows, FFN_GROUP)]
    g2 = g2_ref[layer:layer + 1]
    xs = [x_ref[g, :] for g in groups]
    hs = [_rmsnorm(x, g2).astype(bf16) for x in xs]
    hids = []
    for h in hs:
        gate = _dot(h, wg_ref[...])
        up = _dot(h, wu_ref[...])
        hids.append((gate * jax.nn.sigmoid(gate) * up).astype(bf16))
    for g, x, hid in zip(groups, xs, hids):
        y = x + _dot(hid, wd_ref[...])
        if final_norm:
            y = _rmsnorm(y, gf_ref[...])
        out_ref[g, :] = y


def _transpose_cast_kernel(in_ref, out_ref):
    out_ref[...] = in_ref[0].T.astype(out_ref.dtype)


def _resident(shape, index=None):
    index = (0,) * len(shape) if index is None else index
    return pl.BlockSpec(shape, lambda *_: index, pipeline_mode=pl.Buffered(1))


def _layer_block(stacked, layer, cols=None, col_block=0):
    _, n_rows, n_cols = stacked.shape
    cols = n_cols if cols is None else cols
    return _resident((None, n_rows, cols), (layer, 0, col_block))


def _block_bytes(spec, dtype):
    n = 1
    for d in spec.block_shape:
        n *= 1 if d is None else d
    return n * jnp.dtype(dtype).itemsize


def _mixer(x, layer, g1, w_in, wb, wfg2, bfg, gn, woa, cw, cb, wob, wo):
    b_sz, s_len, _ = x.shape
    rows = MIX_ROWS
    assert s_len % rows == 0 and rows % CUM_ROWS == 0 and CUM_ROWS % CHUNK == 0
    assert _A_WIDTH % _FZ_PAD == 0
    operands = (g1, w_in, w_in, wb, wfg2, bfg, gn, woa, cw, cb, wob, wo)
    specs = [
        _resident(g1.shape),
        _layer_block(w_in, layer, _A_WIDTH),
        _layer_block(w_in, layer, _FZ_PAD, _A_WIDTH // _FZ_PAD),
        _layer_block(wb, layer),
        _layer_block(wfg2, layer),
        _resident(bfg.shape), _resident(gn.shape),
        _layer_block(woa, layer),
        _resident(cw.shape), _resident(cb.shape),
        _layer_block(wob, layer), _layer_block(wo, layer),
    ]
    resident_bytes = sum(_block_bytes(s, a.dtype) for s, a in zip(specs, operands))
    tile = pl.BlockSpec((1, rows, D_MODEL), lambda b, s: (b, s, 0))
    scratch = [
        pltpu.VMEM((GLA_HEADS, HEAD_V, HEAD_K), jnp.float32),
        pltpu.VMEM((rows, KEY_WIDTH), jnp.bfloat16),
        pltpu.VMEM((rows, KEY_WIDTH), jnp.bfloat16),
        pltpu.VMEM((rows, VAL_WIDTH), jnp.bfloat16),
        pltpu.VMEM((rows, KEY_WIDTH), jnp.float32),
        pltpu.VMEM((rows, VAL_WIDTH), jnp.float32),
        pltpu.VMEM((rows + V7X_SUBLANES, CONV_CH), jnp.float32),
        pltpu.VMEM((rows // CHUNK * GLA_HEADS, HEAD_K, HEAD_V), jnp.bfloat16),
    ]
    scratch_bytes = (GLA_HEADS * HEAD_V * HEAD_K * 4 + 2 * rows * KEY_WIDTH * 2
                     + rows * VAL_WIDTH * 2 + rows * KEY_WIDTH * 4 + rows * VAL_WIDTH * 4
                     + (rows + V7X_SUBLANES) * CONV_CH * 4
                     + rows // CHUNK * GLA_HEADS * HEAD_K * HEAD_V * 2)
    return pl.pallas_call(
        functools.partial(_mixer_kernel, layer=layer),
        grid=(b_sz, s_len // rows),
        in_specs=[tile] + specs,
        out_specs=tile,
        out_shape=jax.ShapeDtypeStruct(x.shape, x.dtype),
        scratch_shapes=scratch,
        compiler_params=pltpu.CompilerParams(
            dimension_semantics=("arbitrary", "arbitrary"),
            vmem_limit_bytes=_vmem_limit(resident_bytes + scratch_bytes,
                                         2 * rows * D_MODEL * 4)),
        name="mixer",
    )(x, *operands)


def _ffn(x2d, layer, g2, wg, wu, wd, gf, final_norm):
    n_tok, _ = x2d.shape
    rows = FFN_ROWS
    assert n_tok % rows == 0
    operands = (g2, wg, wu, wd, gf)
    specs = [_resident(g2.shape), _layer_block(wg, layer), _layer_block(wu, layer),
             _layer_block(wd, layer), _resident(gf.shape)]
    resident_bytes = sum(_block_bytes(s, a.dtype) for s, a in zip(specs, operands))
    tile = pl.BlockSpec((rows, D_MODEL), lambda i: (i, 0))
    return pl.pallas_call(
        functools.partial(_ffn_kernel, layer=layer, final_norm=final_norm),
        grid=(n_tok // rows,),
        in_specs=[tile] + specs,
        out_specs=tile,
        out_shape=jax.ShapeDtypeStruct(x2d.shape, x2d.dtype),
        compiler_params=pltpu.CompilerParams(
            dimension_semantics=("arbitrary",),
            vmem_limit_bytes=_vmem_limit(resident_bytes, 2 * rows * D_MODEL * 4)),
        name="ffn_final" if final_norm else "ffn",
    )(x2d, *operands)


def _transposed_columns(w_t, start, width, blk):
    depth, _, n = w_t.shape
    assert width % blk == 0 and start % V7X_SUBLANES == 0 and blk % V7X_LANES == 0
    return pl.pallas_call(
        _transpose_cast_kernel,
        grid=(depth, width // blk),
        in_specs=[pl.BlockSpec((pl.Element(1), pl.Element(blk), pl.Element(n)),
                               lambda l, j: (l, pl.multiple_of(start + j * blk, V7X_SUBLANES), 0))],
        out_specs=pl.BlockSpec((None, n, blk), lambda l, j: (l, 0, j)),
        out_shape=jax.ShapeDtypeStruct((depth, n, width), jnp.bfloat16),
        compiler_params=pltpu.CompilerParams(
            dimension_semantics=("arbitrary", "arbitrary"),
            vmem_limit_bytes=_vmem_limit(0, n * blk * 6)),
        name="transpose_cast",
    )(w_t)


def kernel(x, norm1_g, w_in, w_fg2, b_fg, gla_norm_g, w_oa, conv_w, conv_b, w_ob, w_o,
           norm2_g, w_ffn_gate, w_ffn_up, w_ffn_down, final_g):
    b_sz, s_len, d = x.shape
    depth = w_in.shape[0]
    bf16 = jnp.bfloat16
    w_in_t = jnp.swapaxes(w_in, 1, 2)
    w_in_bf = _transposed_columns(w_in_t, 0, _A_WIDTH + _FZ_PAD, PREP_COLS_A)
    wb = _transposed_columns(w_in_t, _A_WIDTH + GATE_RANK, _B_WIDTH, PREP_COLS_B)
    wfg2 = jnp.pad(w_fg2, ((0, 0), (0, _FZ_PAD - GATE_RANK), (0, 0))).astype(bf16)
    woa, wob, wo = w_oa.astype(bf16), w_ob.astype(bf16), w_o.astype(bf16)
    wg, wu, wd = w_ffn_gate.astype(bf16), w_ffn_up.astype(bf16), w_ffn_down.astype(bf16)
    gf = final_g[None]
    for l in range(depth):
        x = _mixer(x, l, norm1_g, w_in_bf, wb, wfg2, b_fg, gla_norm_g, woa,
                   conv_w, conv_b, wob, wo)
        x = _ffn(x.reshape(b_sz * s_len, d), l, norm2_g, wg, wu, wd, gf,
                 final_norm=(l == depth - 1)).reshape(b_sz, s_len, d)
    return x
```

```python
import functools

import jax
import jax.numpy as jnp
from jax import lax
from jax.experimental import pallas as pl
from jax.experimental.pallas import tpu as pltpu

D_MODEL = 1024
CHUNK = 64
GLA_HEADS = 4
KEY_WIDTH = 512
VAL_WIDTH = 1024
HEAD_K = KEY_WIDTH // GLA_HEADS
HEAD_V = VAL_WIDTH // GLA_HEADS
GATE_RANK = 16
GATE_TAU = 16.0
CONV_CH = 1024
CONV_WIDTH = 3
NORM_EPS = 1e-6

V7X_LANES = 128
V7X_SUBLANES = 8
V7X_MXU_DIM = 256
V7X_VMEM_BYTES = 64 * 1024 * 1024

_A_WIDTH = 2 * KEY_WIDTH + 2 * VAL_WIDTH
_B_WIDTH = 3 * CONV_CH + 2 * D_MODEL
_FZ_PAD = V7X_LANES

MIX_ROWS = 512
MIX_STEP_ROWS = 1024
FFN_ROWS = 1024
FFN_GROUP = 512
CUM_ROWS = V7X_MXU_DIM
PREP_COLS_A = 640
PREP_COLS_B = 1024


def _vmem_limit(resident_bytes, tile_bytes):
    want = resident_bytes + 2 * tile_bytes + 24 * 1024 * 1024
    return int(min(want, V7X_VMEM_BYTES - 6 * 1024 * 1024))


def _rmsnorm(x, g):
    ms = jnp.mean(x * x, axis=-1, keepdims=True)
    return x * lax.rsqrt(ms + NORM_EPS) * g


def _log_sigmoid(x):
    return jnp.minimum(x, 0.0) - jnp.log1p(jnp.exp(-jnp.abs(x)))


def _dot(a, b):
    return jnp.dot(a, b, preferred_element_type=jnp.float32)


def _mixer_kernel(x_ref, g1_ref, wa_ref, wfz_ref, wb_ref, wfg2_ref, bfg_ref,
                  gn_ref, woa_ref, cw_ref, cb_ref, wob_ref, wo_ref,
                  out_ref,
                  state_ref, q_ref, kd_ref, v_ref, lsum_ref, o_ref, u_ref, st_ref, *, layer):
    @pl.when(pl.program_id(1) == 0)
    def _():
        state_ref[...] = jnp.zeros_like(state_ref)
        u_ref[pl.ds(0, V7X_SUBLANES), :] = jnp.zeros((V7X_SUBLANES, CONV_CH), jnp.float32)

    refs = (g1_ref, wa_ref, wfz_ref, wb_ref, wfg2_ref, bfg_ref, gn_ref, woa_ref, cw_ref,
            cb_ref, wob_ref, wo_ref, state_ref, q_ref, kd_ref, v_ref, lsum_ref, o_ref,
            u_ref, st_ref)
    for start in range(0, x_ref.shape[1], MIX_ROWS):
        _mixer_tile(x_ref, out_ref, slice(start, start + MIX_ROWS), *refs, layer=layer)


def _mixer_tile(x_ref, out_ref, tile_rows, g1_ref, wa_ref, wfz_ref, wb_ref, wfg2_ref, bfg_ref,
                gn_ref, woa_ref, cw_ref, cb_ref, wob_ref, wo_ref,
                state_ref, q_ref, kd_ref, v_ref, lsum_ref, o_ref, u_ref, st_ref, *, layer):
    rows = MIX_ROWS
    bf16 = jnp.bfloat16
    row = slice(layer, layer + 1)

    x = x_ref[0, tile_rows, :]
    h = _rmsnorm(x, g1_ref[row]).astype(bf16)

    n_chunks = rows // CHUNK
    chunk_rows = [slice(c * CHUNK, (c + 1) * CHUNK) for c in range(n_chunks)]
    k_cols = [slice(hd * HEAD_K, (hd + 1) * HEAD_K) for hd in range(GLA_HEADS)]
    v_cols = [slice(hd * HEAD_V, (hd + 1) * HEAD_V) for hd in range(GLA_HEADS)]
    col_b, col_c, col_x, col_ga, col_gb = (
        slice(i * CONV_CH, (i + 1) * CONV_CH) for i in range(5))

    fz = _dot(h, wfz_ref[...]).astype(bf16)
    q_ref[...] = (_dot(h, wa_ref[:, 0:KEY_WIDTH]) * (HEAD_K ** -0.5)).astype(bf16)
    fg = _dot(fz, wfg2_ref[...]) + bfg_ref[row]
    k = _dot(h, wa_ref[:, KEY_WIDTH:2 * KEY_WIDTH])
    v_ref[...] = _dot(h, wa_ref[:, 2 * KEY_WIDTH:2 * KEY_WIDTH + VAL_WIDTH]).astype(bf16)
    log_a = _log_sigmoid(fg) * (1.0 / GATE_TAU)
    la_hi = log_a.astype(bf16)
    la_lo = (log_a - la_hi.astype(jnp.float32)).astype(bf16)

    u = _dot(h, wb_ref[:, col_c]) * _dot(h, wb_ref[:, col_x])
    u_ref[pl.ds(V7X_SUBLANES, rows), :] = u
    u1 = u_ref[pl.ds(V7X_SUBLANES - 1, rows), :]
    u2 = u_ref[pl.ds(V7X_SUBLANES - 2, rows), :]
    cw = cw_ref[layer]
    conv = cw[0:1] * u2 + cw[1:2] * u1 + cw[2:3] * u + cb_ref[row]
    u_ref[pl.ds(0, V7X_SUBLANES), :] = u[rows - V7X_SUBLANES:rows]

    ri = lax.broadcasted_iota(jnp.int32, (CUM_ROWS, CUM_ROWS), 0)
    ci = lax.broadcasted_iota(jnp.int32, (CUM_ROWS, CUM_ROWS), 1)
    later = jnp.where((ri // CHUNK == ci // CHUNK) & (ci > ri), 1.0, 0.0).astype(bf16)
    rev = jnp.concatenate(
        [_dot(later, la_hi[i:i + CUM_ROWS]) + _dot(later, la_lo[i:i + CUM_ROWS])
         for i in range(0, rows, CUM_ROWS)], axis=0)
    yb_in = (_dot(h, wb_ref[:, col_b]) * conv).astype(bf16)
    kd_ref[...] = (k * jnp.exp(rev)).astype(bf16)
    lsum_ref[...] = rev + log_a

    kv = [[lax.dot_general(v_ref[rs, v_cols[hd]], kd_ref[rs, k_cols[hd]],
                           (((0,), (0,)), ((), ())), preferred_element_type=jnp.float32)
           for hd in range(GLA_HEADS)] for rs in chunk_rows]
    y_b = _dot(yb_in, wob_ref[...])

    states = [state_ref[hd] for hd in range(GLA_HEADS)]
    for c in range(n_chunks):
        gamma = jnp.exp(lsum_ref[c * CHUNK:c * CHUNK + 1, :])
        for hd in range(GLA_HEADS):
            states[hd] = states[hd] * gamma[:, k_cols[hd]] + kv[c][hd]
        for hd in range(GLA_HEADS):
            st_ref[c * GLA_HEADS + hd] = states[hd].astype(bf16).T
    for hd in range(GLA_HEADS):
        state_ref[hd] = states[hd]
    gated_b = jax.nn.sigmoid(_dot(h, wb_ref[:, col_gb])) * y_b

    r_base = 2 * KEY_WIDTH + VAL_WIDTH
    ga_base = col_ga.start
    fill_w = D_MODEL * 2 // n_chunks
    r_parts, ga_parts = [], []
    for c, rs in enumerate(chunk_rows):
        for hd in range(GLA_HEADS):
            o_ref[rs, v_cols[hd]] = _dot(q_ref[rs, k_cols[hd]], st_ref[c * GLA_HEADS + hd])
        lo = c * fill_w
        if lo < D_MODEL:
            r_parts.append(_dot(h, wa_ref[:, r_base + lo:r_base + lo + fill_w]))
        else:
            lo -= D_MODEL
            ga_parts.append(_dot(h, wb_ref[:, ga_base + lo:ga_base + lo + fill_w]))
    r = jnp.concatenate(r_parts, axis=1)
    g_a = jax.nn.sigmoid(jnp.concatenate(ga_parts, axis=1))

    gn = gn_ref[row]
    parts = []
    for hd in range(GLA_HEADS):
        o_h = o_ref[:, v_cols[hd]]
        ms = jnp.mean(o_h * o_h, axis=-1, keepdims=True)
        parts.append(o_h * lax.rsqrt(ms + NORM_EPS) * gn)
    o_n = jnp.concatenate(parts, axis=1)
    y_a = _dot((o_n * (r * jax.nn.sigmoid(r))).astype(bf16), woa_ref[...])

    mix = (g_a * y_a + gated_b).astype(bf16)
    out_ref[0, tile_rows, :] = x + _dot(mix, wo_ref[...])


def _ffn_kernel(x_ref, g2_ref, wg_ref, wu_ref, wd_ref, gf_ref, out_ref, *, layer, final_norm):
    bf16 = jnp.bfloat16
    rows = x_ref.shape[0]
    groups = [slice(i, i + FFN_GROUP) for i in range(0, rows, FFN_GROUP)]
    g2 = g2_ref[layer:layer + 1]
    xs = [x_ref[g, :] for g in groups]
    hs = [_rmsnorm(x, g2).astype(bf16) for x in xs]
    hids = []
    for h in hs:
        gate = _dot(h, wg_ref[...])
        up = _dot(h, wu_ref[...])
        hids.append((gate * jax.nn.sigmoid(gate) * up).astype(bf16))
    for g, x, hid in zip(groups, xs, hids):
        y = x + _dot(hid, wd_ref[...])
        if final_norm:
            y = _rmsnorm(y, gf_ref[...])
        out_ref[g, :] = y


def _transpose_cast_kernel(in_ref, out_ref):
    out_ref[...] = in_ref[0].T.astype(out_ref.dtype)


def _resident(shape, index=None):
    index = (0,) * len(shape) if index is None else index
    return pl.BlockSpec(shape, lambda *_: index, pipeline_mode=pl.Buffered(1))


def _layer_block(stacked, layer, cols=None, col_block=0):
    _, n_rows, n_cols = stacked.shape
    cols = n_cols if cols is None else cols
    return _resident((None, n_rows, cols), (layer, 0, col_block))


def _block_bytes(spec, dtype):
    n = 1
    for d in spec.block_shape:
        n *= 1 if d is None else d
    return n * jnp.dtype(dtype).itemsize


def _mixer(x, layer, g1, w_in, wb, wfg2, bfg, gn, woa, cw, cb, wob, wo):
    b_sz, s_len, _ = x.shape
    rows = MIX_ROWS
    step_rows = MIX_STEP_ROWS
    assert s_len % step_rows == 0 and step_rows % rows == 0
    assert rows % CUM_ROWS == 0 and CUM_ROWS % CHUNK == 0
    assert _A_WIDTH % _FZ_PAD == 0
    operands = (g1, w_in, w_in, wb, wfg2, bfg, gn, woa, cw, cb, wob, wo)
    specs = [
        _resident(g1.shape),
        _layer_block(w_in, layer, _A_WIDTH),
        _layer_block(w_in, layer, _FZ_PAD, _A_WIDTH // _FZ_PAD),
        _layer_block(wb, layer),
        _layer_block(wfg2, layer),
        _resident(bfg.shape), _resident(gn.shape),
        _layer_block(woa, layer),
        _resident(cw.shape), _resident(cb.shape),
        _layer_block(wob, layer), _layer_block(wo, layer),
    ]
    resident_bytes = sum(_block_bytes(s, a.dtype) for s, a in zip(specs, operands))
    tile = pl.BlockSpec((1, step_rows, D_MODEL), lambda b, s: (b, s, 0))
    scratch = [
        pltpu.VMEM((GLA_HEADS, HEAD_V, HEAD_K), jnp.float32),
        pltpu.VMEM((rows, KEY_WIDTH), jnp.bfloat16),
        pltpu.VMEM((rows, KEY_WIDTH), jnp.bfloat16),
        pltpu.VMEM((rows, VAL_WIDTH), jnp.bfloat16),
        pltpu.VMEM((rows, KEY_WIDTH), jnp.float32),
        pltpu.VMEM((rows, VAL_WIDTH), jnp.float32),
        pltpu.VMEM((rows + V7X_SUBLANES, CONV_CH), jnp.float32),
        pltpu.VMEM((rows // CHUNK * GLA_HEADS, HEAD_K, HEAD_V), jnp.bfloat16),
    ]
    scratch_bytes = (GLA_HEADS * HEAD_V * HEAD_K * 4 + 2 * rows * KEY_WIDTH * 2
                     + rows * VAL_WIDTH * 2 + rows * KEY_WIDTH * 4 + rows * VAL_WIDTH * 4
                     + (rows + V7X_SUBLANES) * CONV_CH * 4
                     + rows // CHUNK * GLA_HEADS * HEAD_K * HEAD_V * 2)
    return pl.pallas_call(
        functools.partial(_mixer_kernel, layer=layer),
        grid=(b_sz, s_len // step_rows),
        in_specs=[tile] + specs,
        out_specs=tile,
        out_shape=jax.ShapeDtypeStruct(x.shape, x.dtype),
        scratch_shapes=scratch,
        compiler_params=pltpu.CompilerParams(
            dimension_semantics=("arbitrary", "arbitrary"),
            vmem_limit_bytes=_vmem_limit(resident_bytes + scratch_bytes,
                                         2 * step_rows * D_MODEL * 4)),
        name="mixer",
    )(x, *operands)


def _ffn(x2d, layer, g2, wg, wu, wd, gf, final_norm):
    n_tok, _ = x2d.shape
    rows = FFN_ROWS
    assert n_tok % rows == 0
    operands = (g2, wg, wu, wd, gf)
    specs = [_resident(g2.shape), _layer_block(wg, layer), _layer_block(wu, layer),
             _layer_block(wd, layer), _resident(gf.shape)]
    resident_bytes = sum(_block_bytes(s, a.dtype) for s, a in zip(specs, operands))
    tile = pl.BlockSpec((rows, D_MODEL), lambda i: (i, 0))
    return pl.pallas_call(
        functools.partial(_ffn_kernel, layer=layer, final_norm=final_norm),
        grid=(n_tok // rows,),
        in_specs=[tile] + specs,
        out_specs=tile,
        out_shape=jax.ShapeDtypeStruct(x2d.shape, x2d.dtype),
        compiler_params=pltpu.CompilerParams(
            dimension_semantics=("arbitrary",),
            vmem_limit_bytes=_vmem_limit(resident_bytes, 2 * rows * D_MODEL * 4)),
        name="ffn_final" if final_norm else "ffn",
    )(x2d, *operands)


def _transposed_columns(w_t, start, width, blk):
    depth, _, n = w_t.shape
    assert width % blk == 0 and start % V7X_SUBLANES == 0 and blk % V7X_LANES == 0
    return pl.pallas_call(
        _transpose_cast_kernel,
        grid=(depth, width // blk),
        in_specs=[pl.BlockSpec((pl.Element(1), pl.Element(blk), pl.Element(n)),
                               lambda l, j: (l, pl.multiple_of(start + j * blk, V7X_SUBLANES), 0))],
        out_specs=pl.BlockSpec((None, n, blk), lambda l, j: (l, 0, j)),
        out_shape=jax.ShapeDtypeStruct((depth, n, width), jnp.bfloat16),
        compiler_params=pltpu.CompilerParams(
            dimension_semantics=("arbitrary", "arbitrary"),
            vmem_limit_bytes=_vmem_limit(0, n * blk * 6)),
        name="transpose_cast",
    )(w_t)


def kernel(x, norm1_g, w_in, w_fg2, b_fg, gla_norm_g, w_oa, conv_w, conv_b, w_ob, w_o,
           norm2_g, w_ffn_gate, w_ffn_up, w_ffn_down, final_g):
    b_sz, s_len, d = x.shape
    depth = w_in.shape[0]
    bf16 = jnp.bfloat16
    w_in_t = jnp.swapaxes(w_in, 1, 2)
    w_in_bf = _transposed_columns(w_in_t, 0, _A_WIDTH + _FZ_PAD, PREP_COLS_A)
    wb = _transposed_columns(w_in_t, _A_WIDTH + GATE_RANK, _B_WIDTH, PREP_COLS_B)
    wfg2 = jnp.pad(w_fg2, ((0, 0), (0, _FZ_PAD - GATE_RANK), (0, 0))).astype(bf16)
    woa, wob, wo = w_oa.astype(bf16), w_ob.astype(bf16), w_o.astype(bf16)
    wg, wu, wd = w_ffn_gate.astype(bf16), w_ffn_up.astype(bf16), w_ffn_down.astype(bf16)
    gf = final_g[None]
    for l in range(depth):
        x = _mixer(x, l, norm1_g, w_in_bf, wb, wfg2, b_fg, gla_norm_g, woa,
                   conv_w, conv_b, wob, wo)
        x = _ffn(x.reshape(b_sz * s_len, d), l, norm2_g, wg, wu, wd, gf,
                 final_norm=(l == depth - 1)).reshape(b_sz, s_len, d)
    return x
```

```python
import functools

import jax
import jax.numpy as jnp
from jax import lax
from jax.experimental import pallas as pl
from jax.experimental.pallas import tpu as pltpu

D_MODEL = 1024
CHUNK = 64
GLA_HEADS = 4
KEY_WIDTH = 512
VAL_WIDTH = 1024
HEAD_K = KEY_WIDTH // GLA_HEADS
HEAD_V = VAL_WIDTH // GLA_HEADS
GATE_RANK = 16
GATE_TAU = 16.0
CONV_CH = 1024
CONV_WIDTH = 3
NORM_EPS = 1e-6

V7X_LANES = 128
V7X_SUBLANES = 8
V7X_MXU_DIM = 256
V7X_VMEM_BYTES = 64 * 1024 * 1024

_A_WIDTH = 2 * KEY_WIDTH + 2 * VAL_WIDTH
_B_WIDTH = 3 * CONV_CH + 2 * D_MODEL
_FZ_PAD = V7X_LANES

MIX_ROWS = 512
FFN_ROWS = 1024
FFN_GROUP = 512
CUM_ROWS = V7X_MXU_DIM
PREP_COLS_A = 640
PREP_COLS_B = 1024


def _vmem_limit(resident_bytes, tile_bytes):
    want = resident_bytes + 2 * tile_bytes + 24 * 1024 * 1024
    return int(min(want, V7X_VMEM_BYTES - 6 * 1024 * 1024))


def _rmsnorm(x, g):
    ms = jnp.mean(x * x, axis=-1, keepdims=True)
    return x * lax.rsqrt(ms + NORM_EPS) * g


def _log_sigmoid(x):
    return jnp.minimum(x, 0.0) - jnp.log1p(jnp.exp(-jnp.abs(x)))


def _sigmoid(x):
    return 0.5 * jnp.tanh(0.5 * x) + 0.5


def _dot(a, b):
    return jnp.dot(a, b, preferred_element_type=jnp.float32)


def _mixer_kernel(x_ref, g1_ref, wa_ref, wfz_ref, wb_ref, wfg2_ref, bfg_ref,
                  gn_ref, woa_ref, cw_ref, cb_ref, wob_ref, wo_ref,
                  out_ref,
                  state_ref, q_ref, kd_ref, v_ref, lsum_ref, o_ref, u_ref, st_ref, *, layer):
    rows = x_ref.shape[1]
    bf16 = jnp.bfloat16
    row = slice(layer, layer + 1)

    @pl.when(pl.program_id(1) == 0)
    def _():
        state_ref[...] = jnp.zeros_like(state_ref)
        u_ref[pl.ds(0, V7X_SUBLANES), :] = jnp.zeros((V7X_SUBLANES, CONV_CH), jnp.float32)

    x = x_ref[0]
    h = _rmsnorm(x, g1_ref[row]).astype(bf16)

    n_chunks = rows // CHUNK
    chunk_rows = [slice(c * CHUNK, (c + 1) * CHUNK) for c in range(n_chunks)]
    k_cols = [slice(hd * HEAD_K, (hd + 1) * HEAD_K) for hd in range(GLA_HEADS)]
    v_cols = [slice(hd * HEAD_V, (hd + 1) * HEAD_V) for hd in range(GLA_HEADS)]
    col_b, col_c, col_x, col_ga, col_gb = (
        slice(i * CONV_CH, (i + 1) * CONV_CH) for i in range(5))

    fz = _dot(h, wfz_ref[...]).astype(bf16)
    q_ref[...] = (_dot(h, wa_ref[:, 0:KEY_WIDTH]) * (HEAD_K ** -0.5)).astype(bf16)
    fg = _dot(fz, wfg2_ref[...]) + bfg_ref[row]
    k = _dot(h, wa_ref[:, KEY_WIDTH:2 * KEY_WIDTH])
    v_ref[...] = _dot(h, wa_ref[:, 2 * KEY_WIDTH:2 * KEY_WIDTH + VAL_WIDTH]).astype(bf16)
    log_a = _log_sigmoid(fg) * (1.0 / GATE_TAU)
    la_hi = log_a.astype(bf16)
    la_lo = (log_a - la_hi.astype(jnp.float32)).astype(bf16)

    u = _dot(h, wb_ref[:, col_c]) * _dot(h, wb_ref[:, col_x])
    u_ref[pl.ds(V7X_SUBLANES, rows), :] = u
    u1 = u_ref[pl.ds(V7X_SUBLANES - 1, rows), :]
    u2 = u_ref[pl.ds(V7X_SUBLANES - 2, rows), :]
    cw = cw_ref[layer]
    conv = cw[0:1] * u2 + cw[1:2] * u1 + cw[2:3] * u + cb_ref[row]
    u_ref[pl.ds(0, V7X_SUBLANES), :] = u[rows - V7X_SUBLANES:rows]

    ri = lax.broadcasted_iota(jnp.int32, (CUM_ROWS, CUM_ROWS), 0)
    ci = lax.broadcasted_iota(jnp.int32, (CUM_ROWS, CUM_ROWS), 1)
    later = jnp.where((ri // CHUNK == ci // CHUNK) & (ci > ri), 1.0, 0.0).astype(bf16)
    rev = jnp.concatenate(
        [_dot(later, la_hi[i:i + CUM_ROWS]) + _dot(later, la_lo[i:i + CUM_ROWS])
         for i in range(0, rows, CUM_ROWS)], axis=0)
    yb_in = (_dot(h, wb_ref[:, col_b]) * conv).astype(bf16)
    kd_ref[...] = (k * jnp.exp(rev)).astype(bf16)
    lsum_ref[...] = rev + log_a

    kv = [[lax.dot_general(v_ref[rs, v_cols[hd]], kd_ref[rs, k_cols[hd]],
                           (((0,), (0,)), ((), ())), preferred_element_type=jnp.float32)
           for hd in range(GLA_HEADS)] for rs in chunk_rows]
    y_b = _dot(yb_in, wob_ref[...])

    states = [state_ref[hd] for hd in range(GLA_HEADS)]
    for c in range(n_chunks):
        gamma = jnp.exp(lsum_ref[c * CHUNK:c * CHUNK + 1, :])
        for hd in range(GLA_HEADS):
            states[hd] = states[hd] * gamma[:, k_cols[hd]] + kv[c][hd]
        for hd in range(GLA_HEADS):
            st_ref[c * GLA_HEADS + hd] = states[hd].astype(bf16).T
    for hd in range(GLA_HEADS):
        state_ref[hd] = states[hd]
    gated_b = _sigmoid(_dot(h, wb_ref[:, col_gb])) * y_b

    r_base = 2 * KEY_WIDTH + VAL_WIDTH
    ga_base = col_ga.start
    fill_w = D_MODEL * 2 // n_chunks
    r_parts, ga_parts = [], []
    for c, rs in enumerate(chunk_rows):
        for hd in range(GLA_HEADS):
            o_ref[rs, v_cols[hd]] = _dot(q_ref[rs, k_cols[hd]], st_ref[c * GLA_HEADS + hd])
        lo = c * fill_w
        if lo < D_MODEL:
            r_parts.append(_dot(h, wa_ref[:, r_base + lo:r_base + lo + fill_w]))
        else:
            lo -= D_MODEL
            ga_parts.append(_dot(h, wb_ref[:, ga_base + lo:ga_base + lo + fill_w]))
    r = jnp.concatenate(r_parts, axis=1)
    g_a = _sigmoid(jnp.concatenate(ga_parts, axis=1))

    gn = gn_ref[row]
    parts = []
    for hd in range(GLA_HEADS):
        o_h = o_ref[:, v_cols[hd]]
        ms = jnp.mean(o_h * o_h, axis=-1, keepdims=True)
        parts.append(o_h * lax.rsqrt(ms + NORM_EPS) * gn)
    o_n = jnp.concatenate(parts, axis=1)
    y_a = _dot((o_n * (r * _sigmoid(r))).astype(bf16), woa_ref[...])

    mix = (g_a * y_a + gated_b).astype(bf16)
    out_ref[0] = x + _dot(mix, wo_ref[...])


def _ffn_kernel(x_ref, g2_ref, wg_ref, wu_ref, wd_ref, gf_ref, out_ref, *, layer, final_norm):
    bf16 = jnp.bfloat16
    rows = x_ref.shape[0]
    groups = [slice(i, i + FFN_GROUP) for i in range(0, rows, FFN_GROUP)]
    g2 = g2_ref[layer:layer + 1]
    xs = [x_ref[g, :] for g in groups]
    hs = [_rmsnorm(x, g2).astype(bf16) for x in xs]
    hids = []
    for h in hs:
        gate = _dot(h, wg_ref[...])
        up = _dot(h, wu_ref[...])
        hids.append((gate * _sigmoid(gate) * up).astype(bf16))
    for g, x, hid in zip(groups, xs, hids):
        y = x + _dot(hid, wd_ref[...])
        if final_norm:
            y = _rmsnorm(y, gf_ref[...])
        out_ref[g, :] = y


def _transpose_cast_kernel(in_ref, out_ref):
    out_ref[...] = in_ref[0].T.astype(out_ref.dtype)


def _resident(shape, index=None):
    index = (0,) * len(shape) if index is None else index
    return pl.BlockSpec(shape, lambda *_: index, pipeline_mode=pl.Buffered(1))


def _layer_block(stacked, layer, cols=None, col_block=0):
    _, n_rows, n_cols = stacked.shape
    cols = n_cols if cols is None else cols
    return _resident((None, n_rows, cols), (layer, 0, col_block))


def _block_bytes(spec, dtype):
    n = 1
    for d in spec.block_shape:
        n *= 1 if d is None else d
    return n * jnp.dtype(dtype).itemsize


def _mixer(x, layer, g1, w_in, wb, wfg2, bfg, gn, woa, cw, cb, wob, wo):
    b_sz, s_len, _ = x.shape
    rows = MIX_ROWS
    assert s_len % rows == 0 and rows % CUM_ROWS == 0 and CUM_ROWS % CHUNK == 0
    assert _A_WIDTH % _FZ_PAD == 0
    operands = (g1, w_in, w_in, wb, wfg2, bfg, gn, woa, cw, cb, wob, wo)
    specs = [
        _resident(g1.shape),
        _layer_block(w_in, layer, _A_WIDTH),
        _layer_block(w_in, layer, _FZ_PAD, _A_WIDTH // _FZ_PAD),
        _layer_block(wb, layer),
        _layer_block(wfg2, layer),
        _resident(bfg.shape), _resident(gn.shape),
        _layer_block(woa, layer),
        _resident(cw.shape), _resident(cb.shape),
        _layer_block(wob, layer), _layer_block(wo, layer),
    ]
    resident_bytes = sum(_block_bytes(s, a.dtype) for s, a in zip(specs, operands))
    tile = pl.BlockSpec((1, rows, D_MODEL), lambda b, s: (b, s, 0))
    scratch = [
        pltpu.VMEM((GLA_HEADS, HEAD_V, HEAD_K), jnp.float32),
        pltpu.VMEM((rows, KEY_WIDTH), jnp.bfloat16),
        pltpu.VMEM((rows, KEY_WIDTH), jnp.bfloat16),
        pltpu.VMEM((rows, VAL_WIDTH), jnp.bfloat16),
        pltpu.VMEM((rows, KEY_WIDTH), jnp.float32),
        pltpu.VMEM((rows, VAL_WIDTH), jnp.float32),
        pltpu.VMEM((rows + V7X_SUBLANES, CONV_CH), jnp.float32),
        pltpu.VMEM((rows // CHUNK * GLA_HEADS, HEAD_K, HEAD_V), jnp.bfloat16),
    ]
    scratch_bytes = (GLA_HEADS * HEAD_V * HEAD_K * 4 + 2 * rows * KEY_WIDTH * 2
                     + rows * VAL_WIDTH * 2 + rows * KEY_WIDTH * 4 + rows * VAL_WIDTH * 4
                     + (rows + V7X_SUBLANES) * CONV_CH * 4
                     + rows // CHUNK * GLA_HEADS * HEAD_K * HEAD_V * 2)
    return pl.pallas_call(
        functools.partial(_mixer_kernel, layer=layer),
        grid=(b_sz, s_len // rows),
        in_specs=[tile] + specs,
        out_specs=tile,
        out_shape=jax.ShapeDtypeStruct(x.shape, x.dtype),
        scratch_shapes=scratch,
        compiler_params=pltpu.CompilerParams(
            dimension_semantics=("arbitrary", "arbitrary"),
            vmem_limit_bytes=_vmem_limit(resident_bytes + scratch_bytes,
                                         2 * rows * D_MODEL * 4)),
        name="mixer",
    )(x, *operands)


def _ffn(x2d, layer, g2, wg, wu, wd, gf, final_norm):
    n_tok, _ = x2d.shape
    rows = FFN_ROWS
    assert n_tok % rows == 0
    operands = (g2, wg, wu, wd, gf)
    specs = [_resident(g2.shape), _layer_block(wg, layer), _layer_block(wu, layer),
             _layer_block(wd, layer), _resident(gf.shape)]
    resident_bytes = sum(_block_bytes(s, a.dtype) for s, a in zip(specs, operands))
    tile = pl.BlockSpec((rows, D_MODEL), lambda i: (i, 0))
    return pl.pallas_call(
        functools.partial(_ffn_kernel, layer=layer, final_norm=final_norm),
        grid=(n_tok // rows,),
        in_specs=[tile] + specs,
        out_specs=tile,
        out_shape=jax.ShapeDtypeStruct(x2d.shape, x2d.dtype),
        compiler_params=pltpu.CompilerParams(
            dimension_semantics=("arbitrary",),
            vmem_limit_bytes=_vmem_limit(resident_bytes, 2 * rows * D_MODEL * 4)),
        name="ffn_final" if final_norm else "ffn",
    )(x2d, *operands)


def _transposed_columns(w_t, start, width, blk):
    depth, _, n = w_t.shape
    assert width % blk == 0 and start % V7X_SUBLANES == 0 and blk % V7X_LANES == 0
    return pl.pallas_call(
        _transpose_cast_kernel,
        grid=(depth, width // blk),
        in_specs=[pl.BlockSpec((pl.Element(1), pl.Element(blk), pl.Element(n)),
                               lambda l, j: (l, pl.multiple_of(start + j * blk, V7X_SUBLANES), 0))],
        out_specs=pl.BlockSpec((None, n, blk), lambda l, j: (l, 0, j)),
        out_shape=jax.ShapeDtypeStruct((depth, n, width), jnp.bfloat16),
        compiler_params=pltpu.CompilerParams(
            dimension_semantics=("arbitrary", "arbitrary"),
            vmem_limit_bytes=_vmem_limit(0, n * blk * 6)),
        name="transpose_cast",
    )(w_t)


def kernel(x, norm1_g, w_in, w_fg2, b_fg, gla_norm_g, w_oa, conv_w, conv_b, w_ob, w_o,
           norm2_g, w_ffn_gate, w_ffn_up, w_ffn_down, final_g):
    b_sz, s_len, d = x.shape
    depth = w_in.shape[0]
    bf16 = jnp.bfloat16
    w_in_t = jnp.swapaxes(w_in, 1, 2)
    w_in_bf = _transposed_columns(w_in_t, 0, _A_WIDTH + _FZ_PAD, PREP_COLS_A)
    wb = _transposed_columns(w_in_t, _A_WIDTH + GATE_RANK, _B_WIDTH, PREP_COLS_B)
    wfg2 = jnp.pad(w_fg2, ((0, 0), (0, _FZ_PAD - GATE_RANK), (0, 0))).astype(bf16)
    woa, wob, wo = w_oa.astype(bf16), w_ob.astype(bf16), w_o.astype(bf16)
    wg, wu, wd = w_ffn_gate.astype(bf16), w_ffn_up.astype(bf16), w_ffn_down.astype(bf16)
    gf = final_g[None]
    for l in range(depth):
        x = _mixer(x, l, norm1_g, w_in_bf, wb, wfg2, b_fg, gla_norm_g, woa,
                   conv_w, conv_b, wob, wo)
        x = _ffn(x.reshape(b_sz * s_len, d), l, norm2_g, wg, wu, wd, gf,
                 final_norm=(l == depth - 1)).reshape(b_sz, s_len, d)
    return x
```

```python
import functools

import jax
import jax.numpy as jnp
from jax import lax
from jax.experimental import pallas as pl
from jax.experimental.pallas import tpu as pltpu

D_MODEL = 1024
CHUNK = 64
GLA_HEADS = 4
KEY_WIDTH = 512
VAL_WIDTH = 1024
HEAD_K = KEY_WIDTH // GLA_HEADS
HEAD_V = VAL_WIDTH // GLA_HEADS
GATE_RANK = 16
GATE_TAU = 16.0
CONV_CH = 1024
CONV_WIDTH = 3
NORM_EPS = 1e-6

V7X_LANES = 128
V7X_SUBLANES = 8
V7X_MXU_DIM = 256
V7X_VMEM_BYTES = 64 * 1024 * 1024

_A_WIDTH = 2 * KEY_WIDTH + 2 * VAL_WIDTH
_B_WIDTH = 3 * CONV_CH + 2 * D_MODEL
_FZ_PAD = V7X_LANES

MIX_ROWS = 512
FFN_ROWS = 1024
FFN_GROUP = 512
CUM_ROWS = V7X_MXU_DIM
PREP_COLS_A = 640
PREP_COLS_B = 1024
SIDE_COLS_A = 128
SIDE_COLS_B = 256


def _vmem_limit(resident_bytes, tile_bytes):
    want = resident_bytes + 2 * tile_bytes + 24 * 1024 * 1024
    return int(min(want, V7X_VMEM_BYTES - 6 * 1024 * 1024))


def _rmsnorm(x, g):
    ms = jnp.mean(x * x, axis=-1, keepdims=True)
    return x * lax.rsqrt(ms + NORM_EPS) * g


def _log_sigmoid(x):
    return jnp.minimum(x, 0.0) - jnp.log1p(jnp.exp(-jnp.abs(x)))


def _sigmoid(x):
    return 0.5 * jnp.tanh(0.5 * x) + 0.5


def _dot(a, b):
    return jnp.dot(a, b, preferred_element_type=jnp.float32)


def _mixer_kernel(x_ref, g1_ref, wa_ref, wfz_ref, wb_ref, wfg2_ref, bfg_ref,
                  gn_ref, woa_ref, cw_ref, cb_ref, wob_ref, wo_ref,
                  wg_f32_ref, wu_f32_ref, wd_f32_ref,
                  out_ref, wg_ref, wu_ref, wd_ref,
                  state_ref, q_ref, kd_ref, v_ref, lsum_ref, o_ref, u_ref, st_ref, *, layer):
    rows = x_ref.shape[1]
    bf16 = jnp.bfloat16
    row = slice(layer, layer + 1)

    wg_ref[...] = wg_f32_ref[...].astype(bf16)
    wu_ref[...] = wu_f32_ref[...].astype(bf16)
    wd_ref[...] = wd_f32_ref[...].astype(bf16)

    @pl.when(pl.program_id(1) == 0)
    def _():
        state_ref[...] = jnp.zeros_like(state_ref)
        u_ref[pl.ds(0, V7X_SUBLANES), :] = jnp.zeros((V7X_SUBLANES, CONV_CH), jnp.float32)

    x = x_ref[0]
    h = _rmsnorm(x, g1_ref[row]).astype(bf16)

    n_chunks = rows // CHUNK
    chunk_rows = [slice(c * CHUNK, (c + 1) * CHUNK) for c in range(n_chunks)]
    k_cols = [slice(hd * HEAD_K, (hd + 1) * HEAD_K) for hd in range(GLA_HEADS)]
    v_cols = [slice(hd * HEAD_V, (hd + 1) * HEAD_V) for hd in range(GLA_HEADS)]
    col_b, col_c, col_x, col_ga, col_gb = (
        slice(i * CONV_CH, (i + 1) * CONV_CH) for i in range(5))

    fz = _dot(h, wfz_ref[...]).astype(bf16)
    q_ref[...] = (_dot(h, wa_ref[:, 0:KEY_WIDTH]) * (HEAD_K ** -0.5)).astype(bf16)
    fg = _dot(fz, wfg2_ref[...]) + bfg_ref[row]
    k = _dot(h, wa_ref[:, KEY_WIDTH:2 * KEY_WIDTH])
    v_ref[...] = _dot(h, wa_ref[:, 2 * KEY_WIDTH:2 * KEY_WIDTH + VAL_WIDTH]).astype(bf16)
    log_a = _log_sigmoid(fg) * (1.0 / GATE_TAU)
    la_hi = log_a.astype(bf16)
    la_lo = (log_a - la_hi.astype(jnp.float32)).astype(bf16)

    u = _dot(h, wb_ref[:, col_c]) * _dot(h, wb_ref[:, col_x])
    u_ref[pl.ds(V7X_SUBLANES, rows), :] = u
    u1 = u_ref[pl.ds(V7X_SUBLANES - 1, rows), :]
    u2 = u_ref[pl.ds(V7X_SUBLANES - 2, rows), :]
    cw = cw_ref[layer]
    conv = cw[0:1] * u2 + cw[1:2] * u1 + cw[2:3] * u + cb_ref[row]
    u_ref[pl.ds(0, V7X_SUBLANES), :] = u[rows - V7X_SUBLANES:rows]

    ri = lax.broadcasted_iota(jnp.int32, (CUM_ROWS, CUM_ROWS), 0)
    ci = lax.broadcasted_iota(jnp.int32, (CUM_ROWS, CUM_ROWS), 1)
    later = jnp.where((ri // CHUNK == ci // CHUNK) & (ci > ri), 1.0, 0.0).astype(bf16)
    rev = jnp.concatenate(
        [_dot(later, la_hi[i:i + CUM_ROWS]) + _dot(later, la_lo[i:i + CUM_ROWS])
         for i in range(0, rows, CUM_ROWS)], axis=0)
    yb_in = (_dot(h, wb_ref[:, col_b]) * conv).astype(bf16)
    kd_ref[...] = (k * jnp.exp(rev)).astype(bf16)
    lsum_ref[...] = rev + log_a

    kv = [[lax.dot_general(v_ref[rs, v_cols[hd]], kd_ref[rs, k_cols[hd]],
                           (((0,), (0,)), ((), ())), preferred_element_type=jnp.float32)
           for hd in range(GLA_HEADS)] for rs in chunk_rows]
    y_b = _dot(yb_in, wob_ref[...])

    states = [state_ref[hd] for hd in range(GLA_HEADS)]
    for c in range(n_chunks):
        gamma = jnp.exp(lsum_ref[c * CHUNK:c * CHUNK + 1, :])
        for hd in range(GLA_HEADS):
            states[hd] = states[hd] * gamma[:, k_cols[hd]] + kv[c][hd]
        for hd in range(GLA_HEADS):
            st_ref[c * GLA_HEADS + hd] = states[hd].astype(bf16).T
    for hd in range(GLA_HEADS):
        state_ref[hd] = states[hd]
    gated_b = _sigmoid(_dot(h, wb_ref[:, col_gb])) * y_b

    r_base = 2 * KEY_WIDTH + VAL_WIDTH
    ga_base = col_ga.start
    fill_w = D_MODEL * 2 // n_chunks
    r_parts, ga_parts = [], []
    for c, rs in enumerate(chunk_rows):
        for hd in range(GLA_HEADS):
            o_ref[rs, v_cols[hd]] = _dot(q_ref[rs, k_cols[hd]], st_ref[c * GLA_HEADS + hd])
        lo = c * fill_w
        if lo < D_MODEL:
            r_parts.append(_dot(h, wa_ref[:, r_base + lo:r_base + lo + fill_w]))
        else:
            lo -= D_MODEL
            ga_parts.append(_dot(h, wb_ref[:, ga_base + lo:ga_base + lo + fill_w]))
    r = jnp.concatenate(r_parts, axis=1)
    g_a = _sigmoid(jnp.concatenate(ga_parts, axis=1))

    gn = gn_ref[row]
    parts = []
    for hd in range(GLA_HEADS):
        o_h = o_ref[:, v_cols[hd]]
        ms = jnp.mean(o_h * o_h, axis=-1, keepdims=True)
        parts.append(o_h * lax.rsqrt(ms + NORM_EPS) * gn)
    o_n = jnp.concatenate(parts, axis=1)
    y_a = _dot((o_n * (r * _sigmoid(r))).astype(bf16), woa_ref[...])

    mix = (g_a * y_a + gated_b).astype(bf16)
    out_ref[0] = x + _dot(mix, wo_ref[...])


def _ffn_kernel(x_ref, g2_ref, wg_ref, wu_ref, wd_ref, gf_ref, *rest, layer, final_norm):
    bf16 = jnp.bfloat16
    if final_norm:
        (out_ref,) = rest
    else:
        (wina_ref, winb_ref, woa_f32_ref, wob_f32_ref, wo_f32_ref,
         out_ref, wa_ref, wb_ref, woa_ref, wob_ref, wo_ref) = rest
        wa_ref[...] = wina_ref[...].T.astype(bf16)
        wb_ref[...] = winb_ref[0].T.astype(bf16)
        woa_ref[...] = woa_f32_ref[...].astype(bf16)
        wob_ref[...] = wob_f32_ref[...].astype(bf16)
        wo_ref[...] = wo_f32_ref[...].astype(bf16)
    rows = x_ref.shape[0]
    groups = [slice(i, i + FFN_GROUP) for i in range(0, rows, FFN_GROUP)]
    g2 = g2_ref[layer:layer + 1]
    xs = [x_ref[g, :] for g in groups]
    hs = [_rmsnorm(x, g2).astype(bf16) for x in xs]
    hids = []
    for h in hs:
        gate = _dot(h, wg_ref[...])
        up = _dot(h, wu_ref[...])
        hids.append((gate * _sigmoid(gate) * up).astype(bf16))
    for g, x, hid in zip(groups, xs, hids):
        y = x + _dot(hid, wd_ref[...])
        if final_norm:
            y = _rmsnorm(y, gf_ref[...])
        out_ref[g, :] = y


def _transpose_cast_kernel(in_ref, out_ref):
    out_ref[...] = in_ref[0].T.astype(out_ref.dtype)


def _resident(shape, index=None):
    index = (0,) * len(shape) if index is None else index
    return pl.BlockSpec(shape, lambda *_: index, pipeline_mode=pl.Buffered(1))


def _block_bytes(spec, dtype):
    n = 1
    for d in spec.block_shape:
        n *= 1 if d is None else getattr(d, "block_size", d)
    return n * jnp.dtype(dtype).itemsize


def _row_chunks(n_rows, n_steps):
    tile = 2 * V7X_SUBLANES
    chunk = tile
    while n_rows % chunk or n_rows // chunk > n_steps:
        chunk += tile
    return chunk


def _cast_job(stacked, layer, n_steps, step_of):
    _, n_rows, n_cols = stacked.shape
    chunk = _row_chunks(n_rows, n_steps)
    last = n_rows // chunk - 1
    in_spec = pl.BlockSpec((None, chunk, n_cols),
                           lambda *g: (layer, jnp.minimum(step_of(*g), last), 0))
    out_spec = pl.BlockSpec((chunk, n_cols), lambda *g: (jnp.minimum(step_of(*g), last), 0))
    return in_spec, out_spec, jax.ShapeDtypeStruct((n_rows, n_cols), jnp.bfloat16)


def _mixer(x, layer, g1, wa, wb, wfg2, bfg, gn, woa, cw, cb, wob, wo, ffn_f32):
    b_sz, s_len, _ = x.shape
    rows = MIX_ROWS
    assert s_len % rows == 0 and rows % CUM_ROWS == 0 and CUM_ROWS % CHUNK == 0
    assert _A_WIDTH % _FZ_PAD == 0
    n_seq = s_len // rows
    operands = (g1, wa, wa, wb, wfg2, bfg, gn, woa, cw, cb, wob, wo)
    specs = [
        _resident(g1.shape),
        _resident((D_MODEL, _A_WIDTH)),
        _resident((D_MODEL, _FZ_PAD), (0, _A_WIDTH // _FZ_PAD)),
        _resident(wb.shape),
        _resident((None,) + wfg2.shape[1:], (layer, 0, 0)),
        _resident(bfg.shape), _resident(gn.shape),
        _resident(woa.shape),
        _resident(cw.shape), _resident(cb.shape),
        _resident(wob.shape), _resident(wo.shape),
    ]
    resident_bytes = sum(_block_bytes(s, a.dtype) for s, a in zip(specs, operands))
    jobs = [_cast_job(w, layer, b_sz * n_seq, lambda b, s: b * n_seq + s) for w in ffn_f32]
    tile = pl.BlockSpec((1, rows, D_MODEL), lambda b, s: (b, s, 0))
    scratch = [
        pltpu.VMEM((GLA_HEADS, HEAD_V, HEAD_K), jnp.float32),
        pltpu.VMEM((rows, KEY_WIDTH), jnp.bfloat16),
        pltpu.VMEM((rows, KEY_WIDTH), jnp.bfloat16),
        pltpu.VMEM((rows, VAL_WIDTH), jnp.bfloat16),
        pltpu.VMEM((rows, KEY_WIDTH), jnp.float32),
        pltpu.VMEM((rows, VAL_WIDTH), jnp.float32),
        pltpu.VMEM((rows + V7X_SUBLANES, CONV_CH), jnp.float32),
        pltpu.VMEM((rows // CHUNK * GLA_HEADS, HEAD_K, HEAD_V), jnp.bfloat16),
    ]
    scratch_bytes = (GLA_HEADS * HEAD_V * HEAD_K * 4 + 2 * rows * KEY_WIDTH * 2
                     + rows * VAL_WIDTH * 2 + rows * KEY_WIDTH * 4 + rows * VAL_WIDTH * 4
                     + (rows + V7X_SUBLANES) * CONV_CH * 4
                     + rows // CHUNK * GLA_HEADS * HEAD_K * HEAD_V * 2)
    job_bytes = sum(_block_bytes(j[0], jnp.float32) + _block_bytes(j[1], jnp.bfloat16) for j in jobs)
    return pl.pallas_call(
        functools.partial(_mixer_kernel, layer=layer),
        grid=(b_sz, n_seq),
        in_specs=[tile] + specs + [j[0] for j in jobs],
        out_specs=[tile] + [j[1] for j in jobs],
        out_shape=[jax.ShapeDtypeStruct(x.shape, x.dtype)] + [j[2] for j in jobs],
        scratch_shapes=scratch,
        compiler_params=pltpu.CompilerParams(
            dimension_semantics=("arbitrary", "arbitrary"),
            vmem_limit_bytes=_vmem_limit(resident_bytes + scratch_bytes,
                                         2 * rows * D_MODEL * 4 + job_bytes)),
        name="mixer",
    )(x, *operands, *ffn_f32)


def _ffn(x2d, layer, g2, wg, wu, wd, gf, final_norm, next_f32=None):
    n_tok, _ = x2d.shape
    rows = FFN_ROWS
    assert n_tok % rows == 0
    n_steps = n_tok // rows
    operands = (g2, wg, wu, wd, gf)
    specs = [_resident(g2.shape), _resident(wg.shape), _resident(wu.shape),
             _resident(wd.shape), _resident(gf.shape)]
    resident_bytes = sum(_block_bytes(s, a.dtype) for s, a in zip(specs, operands))
    tile = pl.BlockSpec((rows, D_MODEL), lambda i: (i, 0))
    extra_in, extra_in_specs, extra_out_specs, extra_out_shapes, job_bytes = [], [], [], [], 0
    if not final_norm:
        w_in_t, w_oa, w_ob, w_o = next_f32
        nxt = layer + 1
        n_a = (_A_WIDTH + _FZ_PAD) // SIDE_COLS_A
        n_b = _B_WIDTH // SIDE_COLS_B
        assert n_a <= n_steps and n_b <= n_steps and SIDE_COLS_A % V7X_LANES == 0
        extra_in_specs += [
            pl.BlockSpec((None, SIDE_COLS_A, D_MODEL),
                         lambda i: (nxt, jnp.minimum(i, n_a - 1), 0)),
            pl.BlockSpec((pl.Element(1), pl.Element(SIDE_COLS_B), pl.Element(D_MODEL)),
                         lambda i: (nxt, pl.multiple_of(
                             _A_WIDTH + GATE_RANK + jnp.minimum(i, n_b - 1) * SIDE_COLS_B,
                             V7X_SUBLANES), 0)),
        ]
        extra_out_specs += [
            pl.BlockSpec((D_MODEL, SIDE_COLS_A), lambda i: (0, jnp.minimum(i, n_a - 1))),
            pl.BlockSpec((D_MODEL, SIDE_COLS_B), lambda i: (0, jnp.minimum(i, n_b - 1))),
        ]
        extra_out_shapes += [jax.ShapeDtypeStruct((D_MODEL, _A_WIDTH + _FZ_PAD), jnp.bfloat16),
                             jax.ShapeDtypeStruct((D_MODEL, _B_WIDTH), jnp.bfloat16)]
        extra_in += [w_in_t, w_in_t]
        job_bytes += (SIDE_COLS_A + SIDE_COLS_B) * D_MODEL * 6
        for w in (w_oa, w_ob, w_o):
            i_spec, o_spec, o_shape = _cast_job(w, nxt, n_steps, lambda i: i)
            extra_in.append(w)
            extra_in_specs.append(i_spec)
            extra_out_specs.append(o_spec)
            extra_out_shapes.append(o_shape)
            job_bytes += _block_bytes(i_spec, jnp.float32) + _block_bytes(o_spec, jnp.bfloat16)
    return pl.pallas_call(
        functools.partial(_ffn_kernel, layer=layer, final_norm=final_norm),
        grid=(n_steps,),
        in_specs=[tile] + specs + extra_in_specs,
        out_specs=[tile] + extra_out_specs,
        out_shape=[jax.ShapeDtypeStruct(x2d.shape, x2d.dtype)] + extra_out_shapes,
        compiler_params=pltpu.CompilerParams(
            dimension_semantics=("arbitrary",),
            vmem_limit_bytes=_vmem_limit(resident_bytes, 2 * rows * D_MODEL * 4 + job_bytes)),
        name="ffn_final" if final_norm else "ffn",
    )(x2d, *operands, *extra_in)


def _transposed_columns(w_t, layer, start, width, blk):
    _, _, n = w_t.shape
    assert width % blk == 0 and start % V7X_SUBLANES == 0 and blk % V7X_LANES == 0
    return pl.pallas_call(
        _transpose_cast_kernel,
        grid=(width // blk,),
        in_specs=[pl.BlockSpec((pl.Element(1), pl.Element(blk), pl.Element(n)),
                               lambda j: (layer, pl.multiple_of(start + j * blk, V7X_SUBLANES), 0))],
        out_specs=pl.BlockSpec((n, blk), lambda j: (0, j)),
        out_shape=jax.ShapeDtypeStruct((n, width), jnp.bfloat16),
        compiler_params=pltpu.CompilerParams(
            dimension_semantics=("arbitrary",),
            vmem_limit_bytes=_vmem_limit(0, n * blk * 6)),
        name="transpose_cast",
    )(w_t)


def kernel(x, norm1_g, w_in, w_fg2, b_fg, gla_norm_g, w_oa, conv_w, conv_b, w_ob, w_o,
           norm2_g, w_ffn_gate, w_ffn_up, w_ffn_down, final_g):
    b_sz, s_len, d = x.shape
    depth = w_in.shape[0]
    bf16 = jnp.bfloat16
    w_in_t = jnp.swapaxes(w_in, 1, 2)
    wa = _transposed_columns(w_in_t, 0, 0, _A_WIDTH + _FZ_PAD, PREP_COLS_A)
    wb = _transposed_columns(w_in_t, 0, _A_WIDTH + GATE_RANK, _B_WIDTH, PREP_COLS_B)
    woa, wob, wo = w_oa[0].astype(bf16), w_ob[0].astype(bf16), w_o[0].astype(bf16)
    wfg2 = jnp.pad(w_fg2, ((0, 0), (0, _FZ_PAD - GATE_RANK), (0, 0))).astype(bf16)
    gf = final_g[None]
    for l in range(depth):
        x, wg, wu, wd = _mixer(x, l, norm1_g, wa, wb, wfg2, b_fg, gla_norm_g, woa,
                               conv_w, conv_b, wob, wo,
                               (w_ffn_gate, w_ffn_up, w_ffn_down))
        last = l == depth - 1
        outs = _ffn(x.reshape(b_sz * s_len, d), l, norm2_g, wg, wu, wd, gf, final_norm=last,
                    next_f32=None if last else (w_in_t, w_oa, w_ob, w_o))
        x = outs[0].reshape(b_sz, s_len, d)
        if not last:
            wa, wb, woa, wob, wo = outs[1:]
    return x
```

```python
import functools

import jax
import jax.numpy as jnp
from jax import lax
from jax.experimental import pallas as pl
from jax.experimental.pallas import tpu as pltpu

D_MODEL = 1024
CHUNK = 64
GLA_HEADS = 4
KEY_WIDTH = 512
VAL_WIDTH = 1024
HEAD_K = KEY_WIDTH // GLA_HEADS
HEAD_V = VAL_WIDTH // GLA_HEADS
GATE_RANK = 16
GATE_TAU = 16.0
CONV_CH = 1024
CONV_WIDTH = 3
NORM_EPS = 1e-6

V7X_LANES = 128
V7X_SUBLANES = 8
V7X_MXU_DIM = 256
V7X_VMEM_BYTES = 64 * 1024 * 1024

_A_WIDTH = 2 * KEY_WIDTH + 2 * VAL_WIDTH
_B_WIDTH = 3 * CONV_CH + 2 * D_MODEL
_FZ_PAD = V7X_LANES

MIX_ROWS = 512
FFN_ROWS = 1024
FFN_GROUP = 256
CUM_ROWS = V7X_MXU_DIM
PREP_COLS_A = 640
PREP_COLS_B = 1024
SIDE_COLS_A = 128
SIDE_COLS_B = 256


def _vmem_limit(resident_bytes, tile_bytes):
    want = resident_bytes + 2 * tile_bytes + 24 * 1024 * 1024
    return int(min(want, V7X_VMEM_BYTES - 6 * 1024 * 1024))


def _rmsnorm(x, g):
    ms = jnp.mean(x * x, axis=-1, keepdims=True)
    return x * lax.rsqrt(ms + NORM_EPS) * g


def _log_sigmoid(x):
    return jnp.minimum(x, 0.0) - jnp.log1p(jnp.exp(-jnp.abs(x)))


def _sigmoid(x):
    return 0.5 * jnp.tanh(0.5 * x) + 0.5


def _silu(x):
    h = 0.5 * x
    return h + h * jnp.tanh(h)


def _dot(a, b):
    return jnp.dot(a, b, preferred_element_type=jnp.float32)


def _mixer_kernel(x_ref, g1_ref, wa_ref, wfz_ref, wb_ref, wfg2_ref, bfg_ref,
                  gn_ref, woa_ref, cw_ref, cb_ref, wob_ref, wo_ref,
                  wg_f32_ref, wu_f32_ref, wd_f32_ref,
                  out_ref, wg_ref, wu_ref, wd_ref,
                  state_ref, q_ref, kd_ref, v_ref, lsum_ref, o_ref, u_ref, st_ref, *, layer):
    rows = x_ref.shape[1]
    bf16 = jnp.bfloat16
    row = slice(layer, layer + 1)

    @pl.when(pl.program_id(1) == 0)
    def _():
        state_ref[...] = jnp.zeros_like(state_ref)
        u_ref[pl.ds(0, V7X_SUBLANES), :] = jnp.zeros((V7X_SUBLANES, CONV_CH), jnp.float32)

    x = x_ref[0]
    h = _rmsnorm(x, g1_ref[row]).astype(bf16)

    n_chunks = rows // CHUNK
    chunk_rows = [slice(c * CHUNK, (c + 1) * CHUNK) for c in range(n_chunks)]
    k_cols = [slice(hd * HEAD_K, (hd + 1) * HEAD_K) for hd in range(GLA_HEADS)]
    v_cols = [slice(hd * HEAD_V, (hd + 1) * HEAD_V) for hd in range(GLA_HEADS)]
    col_b, col_c, col_x, col_ga, col_gb = (
        slice(i * CONV_CH, (i + 1) * CONV_CH) for i in range(5))

    fz = _dot(h, wfz_ref[...]).astype(bf16)
    q_ref[...] = (_dot(h, wa_ref[:, 0:KEY_WIDTH]) * (HEAD_K ** -0.5)).astype(bf16)
    fg = _dot(fz, wfg2_ref[...]) + bfg_ref[row]
    k = _dot(h, wa_ref[:, KEY_WIDTH:2 * KEY_WIDTH])
    v_ref[...] = _dot(h, wa_ref[:, 2 * KEY_WIDTH:2 * KEY_WIDTH + VAL_WIDTH]).astype(bf16)
    log_a = _log_sigmoid(fg) * (1.0 / GATE_TAU)
    la_hi = log_a.astype(bf16)
    la_lo = (log_a - la_hi.astype(jnp.float32)).astype(bf16)

    u = _dot(h, wb_ref[:, col_c]) * _dot(h, wb_ref[:, col_x])
    u_ref[pl.ds(V7X_SUBLANES, rows), :] = u
    u1 = u_ref[pl.ds(V7X_SUBLANES - 1, rows), :]
    u2 = u_ref[pl.ds(V7X_SUBLANES - 2, rows), :]
    cw = cw_ref[layer]
    conv = cw[0:1] * u2 + cw[1:2] * u1 + cw[2:3] * u + cb_ref[row]
    u_ref[pl.ds(0, V7X_SUBLANES), :] = u[rows - V7X_SUBLANES:rows]

    ri = lax.broadcasted_iota(jnp.int32, (CUM_ROWS, CUM_ROWS), 0)
    ci = lax.broadcasted_iota(jnp.int32, (CUM_ROWS, CUM_ROWS), 1)
    later = jnp.where((ri // CHUNK == ci // CHUNK) & (ci > ri), 1.0, 0.0).astype(bf16)
    rev = jnp.concatenate(
        [_dot(later, la_hi[i:i + CUM_ROWS]) + _dot(later, la_lo[i:i + CUM_ROWS])
         for i in range(0, rows, CUM_ROWS)], axis=0)
    yb_in = (_dot(h, wb_ref[:, col_b]) * conv).astype(bf16)
    kd_ref[...] = (k * jnp.exp(rev)).astype(bf16)
    lsum_ref[...] = rev + log_a

    kv = [[lax.dot_general(v_ref[rs, v_cols[hd]], kd_ref[rs, k_cols[hd]],
                           (((0,), (0,)), ((), ())), preferred_element_type=jnp.float32)
           for hd in range(GLA_HEADS)] for rs in chunk_rows]
    y_b = _dot(yb_in, wob_ref[...])

    states = [state_ref[hd] for hd in range(GLA_HEADS)]
    for c in range(n_chunks):
        gamma = jnp.exp(lsum_ref[c * CHUNK:c * CHUNK + 1, :])
        for hd in range(GLA_HEADS):
            states[hd] = states[hd] * gamma[:, k_cols[hd]] + kv[c][hd]
        for hd in range(GLA_HEADS):
            st_ref[c * GLA_HEADS + hd] = states[hd].astype(bf16).T
    for hd in range(GLA_HEADS):
        state_ref[hd] = states[hd]
    gated_b = _sigmoid(_dot(h, wb_ref[:, col_gb])) * y_b

    r_base = 2 * KEY_WIDTH + VAL_WIDTH
    ga_base = col_ga.start
    fill_w = D_MODEL * 2 // n_chunks
    r_parts, ga_parts = [], []
    for c, rs in enumerate(chunk_rows):
        for hd in range(GLA_HEADS):
            o_ref[rs, v_cols[hd]] = _dot(q_ref[rs, k_cols[hd]], st_ref[c * GLA_HEADS + hd])
        lo = c * fill_w
        if lo < D_MODEL:
            r_parts.append(_dot(h, wa_ref[:, r_base + lo:r_base + lo + fill_w]))
        else:
            lo -= D_MODEL
            ga_parts.append(_dot(h, wb_ref[:, ga_base + lo:ga_base + lo + fill_w]))
    r = jnp.concatenate(r_parts, axis=1)
    g_a = _sigmoid(jnp.concatenate(ga_parts, axis=1))

    gn = gn_ref[row]
    parts = []
    for hd in range(GLA_HEADS):
        o_h = o_ref[:, v_cols[hd]]
        ms = jnp.mean(o_h * o_h, axis=-1, keepdims=True)
        parts.append(o_h * lax.rsqrt(ms + NORM_EPS) * gn)
    o_n = jnp.concatenate(parts, axis=1)
    y_a = _dot((o_n * _silu(r)).astype(bf16), woa_ref[...])

    mix = (g_a * y_a + gated_b).astype(bf16)
    out_ref[0] = x + _dot(mix, wo_ref[...])

    wg_ref[...] = wg_f32_ref[...].astype(bf16)
    wu_ref[...] = wu_f32_ref[...].astype(bf16)
    wd_ref[...] = wd_f32_ref[...].astype(bf16)


def _ffn_kernel(x_ref, g2_ref, wg_ref, wu_ref, wd_ref, gf_ref, *rest, layer, final_norm):
    bf16 = jnp.bfloat16
    out_ref = rest[0] if final_norm else rest[5]
    rows = x_ref.shape[0]
    groups = [slice(i, i + FFN_GROUP) for i in range(0, rows, FFN_GROUP)]
    g2 = g2_ref[layer:layer + 1]
    xs = [x_ref[g, :] for g in groups]
    hs = [_rmsnorm(x, g2).astype(bf16) for x in xs]
    hids = []
    for h in hs:
        gate = _dot(h, wg_ref[...])
        up = _dot(h, wu_ref[...])
        hids.append((_silu(gate) * up).astype(bf16))
    for g, x, hid in zip(groups, xs, hids):
        y = x + _dot(hid, wd_ref[...])
        if final_norm:
            y = _rmsnorm(y, gf_ref[...])
        out_ref[g, :] = y

    if not final_norm:
        (wina_ref, winb_ref, woa_f32_ref, wob_f32_ref, wo_f32_ref,
         _, wa_ref, wb_ref, woa_ref, wob_ref, wo_ref) = rest
        wa_ref[...] = wina_ref[...].T.astype(bf16)
        wb_ref[...] = winb_ref[0].T.astype(bf16)
        woa_ref[...] = woa_f32_ref[...].astype(bf16)
        wob_ref[...] = wob_f32_ref[...].astype(bf16)
        wo_ref[...] = wo_f32_ref[...].astype(bf16)


def _transpose_cast_kernel(in_ref, out_ref):
    out_ref[...] = in_ref[0].T.astype(out_ref.dtype)


def _resident(shape, index=None):
    index = (0,) * len(shape) if index is None else index
    return pl.BlockSpec(shape, lambda *_: index, pipeline_mode=pl.Buffered(1))


def _block_bytes(spec, dtype):
    n = 1
    for d in spec.block_shape:
        n *= 1 if d is None else getattr(d, "block_size", d)
    return n * jnp.dtype(dtype).itemsize


def _row_chunks(n_rows, n_steps):
    tile = 2 * V7X_SUBLANES
    chunk = tile
    while n_rows % chunk or n_rows // chunk > n_steps:
        chunk += tile
    return chunk


def _cast_job(stacked, layer, n_steps, step_of):
    _, n_rows, n_cols = stacked.shape
    chunk = _row_chunks(n_rows, n_steps)
    last = n_rows // chunk - 1
    in_spec = pl.BlockSpec((None, chunk, n_cols),
                           lambda *g: (layer, jnp.minimum(step_of(*g), last), 0))
    out_spec = pl.BlockSpec((chunk, n_cols), lambda *g: (jnp.minimum(step_of(*g), last), 0))
    return in_spec, out_spec, jax.ShapeDtypeStruct((n_rows, n_cols), jnp.bfloat16)


def _mixer(x, layer, g1, wa, wb, wfg2, bfg, gn, woa, cw, cb, wob, wo, ffn_f32):
    b_sz, s_len, _ = x.shape
    rows = MIX_ROWS
    assert s_len % rows == 0 and rows % CUM_ROWS == 0 and CUM_ROWS % CHUNK == 0
    assert _A_WIDTH % _FZ_PAD == 0
    n_seq = s_len // rows
    operands = (g1, wa, wa, wb, wfg2, bfg, gn, woa, cw, cb, wob, wo)
    specs = [
        _resident(g1.shape),
        _resident((D_MODEL, _A_WIDTH)),
        _resident((D_MODEL, _FZ_PAD), (0, _A_WIDTH // _FZ_PAD)),
        _resident(wb.shape),
        _resident((None,) + wfg2.shape[1:], (layer, 0, 0)),
        _resident(bfg.shape), _resident(gn.shape),
        _resident(woa.shape),
        _resident(cw.shape), _resident(cb.shape),
        _resident(wob.shape), _resident(wo.shape),
    ]
    resident_bytes = sum(_block_bytes(s, a.dtype) for s, a in zip(specs, operands))
    jobs = [_cast_job(w, layer, b_sz * n_seq, lambda b, s: b * n_seq + s) for w in ffn_f32]
    tile = pl.BlockSpec((1, rows, D_MODEL), lambda b, s: (b, s, 0))
    scratch = [
        pltpu.VMEM((GLA_HEADS, HEAD_V, HEAD_K), jnp.float32),
        pltpu.VMEM((rows, KEY_WIDTH), jnp.bfloat16),
        pltpu.VMEM((rows, KEY_WIDTH), jnp.bfloat16),
        pltpu.VMEM((rows, VAL_WIDTH), jnp.bfloat16),
        pltpu.VMEM((rows, KEY_WIDTH), jnp.float32),
        pltpu.VMEM((rows, VAL_WIDTH), jnp.float32),
        pltpu.VMEM((rows + V7X_SUBLANES, CONV_CH), jnp.float32),
        pltpu.VMEM((rows // CHUNK * GLA_HEADS, HEAD_K, HEAD_V), jnp.bfloat16),
    ]
    scratch_bytes = (GLA_HEADS * HEAD_V * HEAD_K * 4 + 2 * rows * KEY_WIDTH * 2
                     + rows * VAL_WIDTH * 2 + rows * KEY_WIDTH * 4 + rows * VAL_WIDTH * 4
                     + (rows + V7X_SUBLANES) * CONV_CH * 4
                     + rows // CHUNK * GLA_HEADS * HEAD_K * HEAD_V * 2)
    job_bytes = sum(_block_bytes(j[0], jnp.float32) + _block_bytes(j[1], jnp.bfloat16) for j in jobs)
    return pl.pallas_call(
        functools.partial(_mixer_kernel, layer=layer),
        grid=(b_sz, n_seq),
        in_specs=[tile] + specs + [j[0] for j in jobs],
        out_specs=[tile] + [j[1] for j in jobs],
        out_shape=[jax.ShapeDtypeStruct(x.shape, x.dtype)] + [j[2] for j in jobs],
        scratch_shapes=scratch,
        compiler_params=pltpu.CompilerParams(
            dimension_semantics=("arbitrary", "arbitrary"),
            vmem_limit_bytes=_vmem_limit(resident_bytes + scratch_bytes,
                                         2 * rows * D_MODEL * 4 + job_bytes)),
        name="mixer",
    )(x, *operands, *ffn_f32)


def _ffn(x2d, layer, g2, wg, wu, wd, gf, final_norm, next_f32=None):
    n_tok, _ = x2d.shape
    rows = FFN_ROWS
    assert n_tok % rows == 0
    n_steps = n_tok // rows
    operands = (g2, wg, wu, wd, gf)
    specs = [_resident(g2.shape), _resident(wg.shape), _resident(wu.shape),
             _resident(wd.shape), _resident(gf.shape)]
    resident_bytes = sum(_block_bytes(s, a.dtype) for s, a in zip(specs, operands))
    tile = pl.BlockSpec((rows, D_MODEL), lambda i: (i, 0))
    extra_in, extra_in_specs, extra_out_specs, extra_out_shapes, job_bytes = [], [], [], [], 0
    if not final_norm:
        w_in_t, w_oa, w_ob, w_o = next_f32
        nxt = layer + 1
        n_a = (_A_WIDTH + _FZ_PAD) // SIDE_COLS_A
        n_b = _B_WIDTH // SIDE_COLS_B
        assert n_a <= n_steps and n_b <= n_steps and SIDE_COLS_A % V7X_LANES == 0
        extra_in_specs += [
            pl.BlockSpec((None, SIDE_COLS_A, D_MODEL),
                         lambda i: (nxt, jnp.minimum(i, n_a - 1), 0)),
            pl.BlockSpec((pl.Element(1), pl.Element(SIDE_COLS_B), pl.Element(D_MODEL)),
                         lambda i: (nxt, pl.multiple_of(
                             _A_WIDTH + GATE_RANK + jnp.minimum(i, n_b - 1) * SIDE_COLS_B,
                             V7X_SUBLANES), 0)),
        ]
        extra_out_specs += [
            pl.BlockSpec((D_MODEL, SIDE_COLS_A), lambda i: (0, jnp.minimum(i, n_a - 1))),
            pl.BlockSpec((D_MODEL, SIDE_COLS_B), lambda i: (0, jnp.minimum(i, n_b - 1))),
        ]
        extra_out_shapes += [jax.ShapeDtypeStruct((D_MODEL, _A_WIDTH + _FZ_PAD), jnp.bfloat16),
                             jax.ShapeDtypeStruct((D_MODEL, _B_WIDTH), jnp.bfloat16)]
        extra_in += [w_in_t, w_in_t]
        job_bytes += (SIDE_COLS_A + SIDE_COLS_B) * D_MODEL * 6
        for w in (w_oa, w_ob, w_o):
            i_spec, o_spec, o_shape = _cast_job(w, nxt, n_steps, lambda i: i)
            extra_in.append(w)
            extra_in_specs.append(i_spec)
            extra_out_specs.append(o_spec)
            extra_out_shapes.append(o_shape)
            job_bytes += _block_bytes(i_spec, jnp.float32) + _block_bytes(o_spec, jnp.bfloat16)
    return pl.pallas_call(
        functools.partial(_ffn_kernel, layer=layer, final_norm=final_norm),
        grid=(n_steps,),
        in_specs=[tile] + specs + extra_in_specs,
        out_specs=[tile] + extra_out_specs,
        out_shape=[jax.ShapeDtypeStruct(x2d.shape, x2d.dtype)] + extra_out_shapes,
        compiler_params=pltpu.CompilerParams(
            dimension_semantics=("arbitrary",),
            vmem_limit_bytes=_vmem_limit(resident_bytes, 2 * rows * D_MODEL * 4 + job_bytes)),
        name="ffn_final" if final_norm else "ffn",
    )(x2d, *operands, *extra_in)


def _transposed_columns(w_t, layer, start, width, blk):
    _, _, n = w_t.shape
    assert width % blk == 0 and start % V7X_SUBLANES == 0 and blk % V7X_LANES == 0
    return pl.pallas_call(
        _transpose_cast_kernel,
        grid=(width // blk,),
        in_specs=[pl.BlockSpec((pl.Element(1), pl.Element(blk), pl.Element(n)),
                               lambda j: (layer, pl.multiple_of(start + j * blk, V7X_SUBLANES), 0))],
        out_specs=pl.BlockSpec((n, blk), lambda j: (0, j)),
        out_shape=jax.ShapeDtypeStruct((n, width), jnp.bfloat16),
        compiler_params=pltpu.CompilerParams(
            dimension_semantics=("arbitrary",),
            vmem_limit_bytes=_vmem_limit(0, n * blk * 6)),
        name="transpose_cast",
    )(w_t)


def kernel(x, norm1_g, w_in, w_fg2, b_fg, gla_norm_g, w_oa, conv_w, conv_b, w_ob, w_o,
           norm2_g, w_ffn_gate, w_ffn_up, w_ffn_down, final_g):
    b_sz, s_len, d = x.shape
    depth = w_in.shape[0]
    bf16 = jnp.bfloat16
    w_in_t = jnp.swapaxes(w_in, 1, 2)
    wa = _transposed_columns(w_in_t, 0, 0, _A_WIDTH + _FZ_PAD, PREP_COLS_A)
    wb = _transposed_columns(w_in_t, 0, _A_WIDTH + GATE_RANK, _B_WIDTH, PREP_COLS_B)
    woa, wob, wo = w_oa[0].astype(bf16), w_ob[0].astype(bf16), w_o[0].astype(bf16)
    wfg2 = jnp.pad(w_fg2, ((0, 0), (0, _FZ_PAD - GATE_RANK), (0, 0))).astype(bf16)
    gf = final_g[None]
    for l in range(depth):
        x, wg, wu, wd = _mixer(x, l, norm1_g, wa, wb, wfg2, b_fg, gla_norm_g, woa,
                               conv_w, conv_b, wob, wo,
                               (w_ffn_gate, w_ffn_up, w_ffn_down))
        last = l == depth - 1
        outs = _ffn(x.reshape(b_sz * s_len, d), l, norm2_g, wg, wu, wd, gf, final_norm=last,
                    next_f32=None if last else (w_in_t, w_oa, w_ob, w_o))
        x = outs[0].reshape(b_sz, s_len, d)
        if not last:
            wa, wb, woa, wob, wo = outs[1:]
    return x
```

```python
import functools

import jax
import jax.numpy as jnp
from jax import lax
from jax.experimental import pallas as pl
from jax.experimental.pallas import tpu as pltpu

D_MODEL = 1024
CHUNK = 64
GLA_HEADS = 4
KEY_WIDTH = 512
VAL_WIDTH = 1024
HEAD_K = KEY_WIDTH // GLA_HEADS
HEAD_V = VAL_WIDTH // GLA_HEADS
GATE_RANK = 16
GATE_TAU = 16.0
CONV_CH = 1024
CONV_WIDTH = 3
NORM_EPS = 1e-6

V7X_LANES = 128
V7X_SUBLANES = 8
V7X_BF16_ROWS = 16
V7X_MXU_DIM = 256
V7X_VMEM_BYTES = 64 * 1024 * 1024

_A_WIDTH = 2 * KEY_WIDTH + 2 * VAL_WIDTH
_B_WIDTH = 3 * CONV_CH + 2 * D_MODEL
_FZ_PAD = V7X_LANES

MIX_ROWS = 512
FFN_ROWS = 1024
FFN_GROUP = 256
CUM_ROWS = V7X_MXU_DIM
PREP_COLS_A = 640
PREP_COLS_B = 1024
SIDE_COLS_A = 128
SIDE_COLS_B = 256


_VALUE_TEMP_BYTES = 24 * 1024 * 1024
_VMEM_RESERVE_BYTES = 6 * 1024 * 1024


def _vmem_limit(resident_bytes, tile_bytes):
    want = resident_bytes + 2 * tile_bytes + _VALUE_TEMP_BYTES
    return int(min(want, V7X_VMEM_BYTES - _VMEM_RESERVE_BYTES))


def _rmsnorm(x, g):
    ms = jnp.mean(x * x, axis=-1, keepdims=True)
    return x * lax.rsqrt(ms + NORM_EPS) * g


def _log_sigmoid(x):
    return jnp.minimum(x, 0.0) - jnp.log1p(jnp.exp(-jnp.abs(x)))


def _sigmoid(x):
    return 0.5 * jnp.tanh(0.5 * x) + 0.5


def _silu(x):
    h = 0.5 * x
    return h + h * jnp.tanh(h)


def _dot(a, b):
    return jnp.dot(a, b, preferred_element_type=jnp.float32)


def _mixer_kernel(x_ref, g1_ref, wa_ref, wfz_ref, wb_ref, wfg2_ref, bfg_ref,
                  gn_ref, woa_ref, cw_ref, cb_ref, wob_ref, wo_ref,
                  wg_f32_ref, wu_f32_ref, wd_f32_ref,
                  out_ref, wg_ref, wu_ref, wd_ref,
                  state_ref, q_ref, kd_ref, v_ref, lsum_ref, o_ref, u_ref, st_ref, *, layer):
    rows = x_ref.shape[1]
    bf16 = jnp.bfloat16
    row = slice(layer, layer + 1)

    @pl.when(pl.program_id(1) == 0)
    def _():
        state_ref[...] = jnp.zeros_like(state_ref)
        u_ref[pl.ds(0, V7X_SUBLANES), :] = jnp.zeros((V7X_SUBLANES, CONV_CH), jnp.float32)

    x = x_ref[0]
    h = _rmsnorm(x, g1_ref[row]).astype(bf16)

    n_chunks = rows // CHUNK
    chunk_rows = [slice(c * CHUNK, (c + 1) * CHUNK) for c in range(n_chunks)]
    k_cols = [slice(hd * HEAD_K, (hd + 1) * HEAD_K) for hd in range(GLA_HEADS)]
    v_cols = [slice(hd * HEAD_V, (hd + 1) * HEAD_V) for hd in range(GLA_HEADS)]
    col_b, col_c, col_x, col_ga, col_gb = (
        slice(i * CONV_CH, (i + 1) * CONV_CH) for i in range(5))

    fz = _dot(h, wfz_ref[...]).astype(bf16)
    q_ref[...] = (_dot(h, wa_ref[:, 0:KEY_WIDTH]) * (HEAD_K ** -0.5)).astype(bf16)
    fg = _dot(fz, wfg2_ref[...]) + bfg_ref[row]
    k = _dot(h, wa_ref[:, KEY_WIDTH:2 * KEY_WIDTH])
    v_ref[...] = _dot(h, wa_ref[:, 2 * KEY_WIDTH:2 * KEY_WIDTH + VAL_WIDTH]).astype(bf16)
    log_a = _log_sigmoid(fg) * (1.0 / GATE_TAU)
    la_hi = log_a.astype(bf16)
    la_lo = (log_a - la_hi.astype(jnp.float32)).astype(bf16)

    u = _dot(h, wb_ref[:, col_c]) * _dot(h, wb_ref[:, col_x])
    u_ref[pl.ds(V7X_SUBLANES, rows), :] = u
    u1 = u_ref[pl.ds(V7X_SUBLANES - 1, rows), :]
    u2 = u_ref[pl.ds(V7X_SUBLANES - 2, rows), :]
    cw = cw_ref[layer]
    conv = cw[0:1] * u2 + cw[1:2] * u1 + cw[2:3] * u + cb_ref[row]
    u_ref[pl.ds(0, V7X_SUBLANES), :] = u[rows - V7X_SUBLANES:rows]

    ri = lax.broadcasted_iota(jnp.int32, (CUM_ROWS, CUM_ROWS), 0)
    ci = lax.broadcasted_iota(jnp.int32, (CUM_ROWS, CUM_ROWS), 1)
    later = jnp.where((ri // CHUNK == ci // CHUNK) & (ci > ri), 1.0, 0.0).astype(bf16)
    rev = jnp.concatenate(
        [_dot(later, la_hi[i:i + CUM_ROWS]) + _dot(later, la_lo[i:i + CUM_ROWS])
         for i in range(0, rows, CUM_ROWS)], axis=0)
    yb_in = (_dot(h, wb_ref[:, col_b]) * conv).astype(bf16)
    kd_ref[...] = (k * jnp.exp(rev)).astype(bf16)
    lsum_ref[...] = rev + log_a

    kv = [[lax.dot_general(v_ref[rs, v_cols[hd]], kd_ref[rs, k_cols[hd]],
                           (((0,), (0,)), ((), ())), preferred_element_type=jnp.float32)
           for hd in range(GLA_HEADS)] for rs in chunk_rows]
    y_b = _dot(yb_in, wob_ref[...])

    states = [state_ref[hd] for hd in range(GLA_HEADS)]
    for c in range(n_chunks):
        gamma = jnp.exp(lsum_ref[c * CHUNK:c * CHUNK + 1, :])
        for hd in range(GLA_HEADS):
            states[hd] = states[hd] * gamma[:, k_cols[hd]] + kv[c][hd]
        for hd in range(GLA_HEADS):
            st_ref[c * GLA_HEADS + hd] = states[hd].astype(bf16).T
    for hd in range(GLA_HEADS):
        state_ref[hd] = states[hd]
    gated_b = _sigmoid(_dot(h, wb_ref[:, col_gb])) * y_b

    r_base = 2 * KEY_WIDTH + VAL_WIDTH
    ga_base = col_ga.start
    fill_w = D_MODEL * 2 // n_chunks
    r_parts, ga_parts = [], []
    for c, rs in enumerate(chunk_rows):
        for hd in range(GLA_HEADS):
            o_ref[rs, v_cols[hd]] = _dot(q_ref[rs, k_cols[hd]], st_ref[c * GLA_HEADS + hd])
        lo = c * fill_w
        if lo < D_MODEL:
            r_parts.append(_dot(h, wa_ref[:, r_base + lo:r_base + lo + fill_w]))
        else:
            lo -= D_MODEL
            ga_parts.append(_dot(h, wb_ref[:, ga_base + lo:ga_base + lo + fill_w]))
    r = jnp.concatenate(r_parts, axis=1)
    g_a = _sigmoid(jnp.concatenate(ga_parts, axis=1))

    gn = gn_ref[row]
    parts = []
    for hd in range(GLA_HEADS):
        o_h = o_ref[:, v_cols[hd]]
        ms = jnp.mean(o_h * o_h, axis=-1, keepdims=True)
        parts.append(o_h * lax.rsqrt(ms + NORM_EPS) * gn)
    o_n = jnp.concatenate(parts, axis=1)
    y_a = _dot((o_n * _silu(r)).astype(bf16), woa_ref[...])

    mix = (g_a * y_a + gated_b).astype(bf16)
    out_ref[0] = x + _dot(mix, wo_ref[...])

    wg_ref[...] = wg_f32_ref[...].astype(bf16)
    wu_ref[...] = wu_f32_ref[...].astype(bf16)
    wd_ref[...] = wd_f32_ref[...].astype(bf16)


def _ffn_kernel(x_ref, g2_ref, wg_ref, wu_ref, wd_ref, gf_ref, *rest, layer, final_norm):
    bf16 = jnp.bfloat16
    out_ref = rest[0] if final_norm else rest[5]
    rows = x_ref.shape[0]
    groups = [slice(i, i + FFN_GROUP) for i in range(0, rows, FFN_GROUP)]
    g2 = g2_ref[layer:layer + 1]
    xs = [x_ref[g, :] for g in groups]
    hs = [_rmsnorm(x, g2).astype(bf16) for x in xs]
    hids = []
    for h in hs:
        gate = _dot(h, wg_ref[...])
        up = _dot(h, wu_ref[...])
        hids.append((_silu(gate) * up).astype(bf16))
    for g, x, hid in zip(groups, xs, hids):
        y = x + _dot(hid, wd_ref[...])
        if final_norm:
            y = _rmsnorm(y, gf_ref[...])
        out_ref[g, :] = y

    if not final_norm:
        (wina_ref, winb_ref, woa_f32_ref, wob_f32_ref, wo_f32_ref,
         _, wa_ref, wb_ref, woa_ref, wob_ref, wo_ref) = rest
        wa_ref[...] = wina_ref[...].T.astype(bf16)
        wb_ref[...] = winb_ref[0].T.astype(bf16)
        woa_ref[...] = woa_f32_ref[...].astype(bf16)
        wob_ref[...] = wob_f32_ref[...].astype(bf16)
        wo_ref[...] = wo_f32_ref[...].astype(bf16)


def _transpose_cast_kernel(in_ref, out_ref):
    out_ref[...] = in_ref[0].T.astype(out_ref.dtype)


def _resident(shape, index=None):
    index = (0,) * len(shape) if index is None else index
    return pl.BlockSpec(shape, lambda *_: index, pipeline_mode=pl.Buffered(1))


def _block_bytes(spec, dtype):
    n = 1
    for d in spec.block_shape:
        n *= 1 if d is None else getattr(d, "block_size", d)
    return n * jnp.dtype(dtype).itemsize


def _row_chunks(n_rows, n_steps):
    tile = V7X_BF16_ROWS
    chunk = tile
    while n_rows % chunk or n_rows // chunk > n_steps:
        chunk += tile
    return chunk


def _cast_job(stacked, layer, n_steps, step_of):
    _, n_rows, n_cols = stacked.shape
    chunk = _row_chunks(n_rows, n_steps)
    last = n_rows // chunk - 1
    in_spec = pl.BlockSpec((None, chunk, n_cols),
                           lambda *g: (layer, jnp.minimum(step_of(*g), last), 0))
    out_spec = pl.BlockSpec((chunk, n_cols), lambda *g: (jnp.minimum(step_of(*g), last), 0))
    return in_spec, out_spec, jax.ShapeDtypeStruct((n_rows, n_cols), jnp.bfloat16)


def _mixer(x, layer, g1, wa, wb, wfg2, bfg, gn, woa, cw, cb, wob, wo, ffn_f32):
    b_sz, s_len, _ = x.shape
    rows = MIX_ROWS
    assert s_len % rows == 0 and rows % CUM_ROWS == 0 and CUM_ROWS % CHUNK == 0
    assert _A_WIDTH % _FZ_PAD == 0
    n_seq = s_len // rows
    operands = (g1, wa, wa, wb, wfg2, bfg, gn, woa, cw, cb, wob, wo)
    specs = [
        _resident(g1.shape),
        _resident((D_MODEL, _A_WIDTH)),
        _resident((D_MODEL, _FZ_PAD), (0, _A_WIDTH // _FZ_PAD)),
        _resident(wb.shape),
        _resident((None,) + wfg2.shape[1:], (layer, 0, 0)),
        _resident(bfg.shape), _resident(gn.shape),
        _resident(woa.shape),
        _resident(cw.shape), _resident(cb.shape),
        _resident(wob.shape), _resident(wo.shape),
    ]
    resident_bytes = sum(_block_bytes(s, a.dtype) for s, a in zip(specs, operands))
    jobs = [_cast_job(w, layer, b_sz * n_seq, lambda b, s: b * n_seq + s) for w in ffn_f32]
    tile = pl.BlockSpec((1, rows, D_MODEL), lambda b, s: (b, s, 0))
    scratch = [
        pltpu.VMEM((GLA_HEADS, HEAD_V, HEAD_K), jnp.float32),
        pltpu.VMEM((rows, KEY_WIDTH), jnp.bfloat16),
        pltpu.VMEM((rows, KEY_WIDTH), jnp.bfloat16),
        pltpu.VMEM((rows, VAL_WIDTH), jnp.bfloat16),
        pltpu.VMEM((rows, KEY_WIDTH), jnp.float32),
        pltpu.VMEM((rows, VAL_WIDTH), jnp.float32),
        pltpu.VMEM((rows + V7X_SUBLANES, CONV_CH), jnp.float32),
        pltpu.VMEM((rows // CHUNK * GLA_HEADS, HEAD_K, HEAD_V), jnp.bfloat16),
    ]
    scratch_bytes = (GLA_HEADS * HEAD_V * HEAD_K * 4 + 2 * rows * KEY_WIDTH * 2
                     + rows * VAL_WIDTH * 2 + rows * KEY_WIDTH * 4 + rows * VAL_WIDTH * 4
                     + (rows + V7X_SUBLANES) * CONV_CH * 4
                     + rows // CHUNK * GLA_HEADS * HEAD_K * HEAD_V * 2)
    job_bytes = sum(_block_bytes(j[0], jnp.float32) + _block_bytes(j[1], jnp.bfloat16) for j in jobs)
    return pl.pallas_call(
        functools.partial(_mixer_kernel, layer=layer),
        grid=(b_sz, n_seq),
        in_specs=[tile] + specs + [j[0] for j in jobs],
        out_specs=[tile] + [j[1] for j in jobs],
        out_shape=[jax.ShapeDtypeStruct(x.shape, x.dtype)] + [j[2] for j in jobs],
        scratch_shapes=scratch,
        compiler_params=pltpu.CompilerParams(
            dimension_semantics=("arbitrary", "arbitrary"),
            vmem_limit_bytes=_vmem_limit(resident_bytes + scratch_bytes,
                                         2 * rows * D_MODEL * 4 + job_bytes)),
        name="mixer",
    )(x, *operands, *ffn_f32)


def _ffn(x2d, layer, g2, wg, wu, wd, gf, final_norm, next_f32=None):
    n_tok, _ = x2d.shape
    rows = FFN_ROWS
    assert n_tok % rows == 0
    n_steps = n_tok // rows
    operands = (g2, wg, wu, wd, gf)
    specs = [_resident(g2.shape), _resident(wg.shape), _resident(wu.shape),
             _resident(wd.shape), _resident(gf.shape)]
    resident_bytes = sum(_block_bytes(s, a.dtype) for s, a in zip(specs, operands))
    tile = pl.BlockSpec((rows, D_MODEL), lambda i: (i, 0))
    extra_in, extra_in_specs, extra_out_specs, extra_out_shapes, job_bytes = [], [], [], [], 0
    if not final_norm:
        w_in_t, w_oa, w_ob, w_o = next_f32
        nxt = layer + 1
        n_a = (_A_WIDTH + _FZ_PAD) // SIDE_COLS_A
        n_b = _B_WIDTH // SIDE_COLS_B
        assert n_a <= n_steps and n_b <= n_steps and SIDE_COLS_A % V7X_LANES == 0
        extra_in_specs += [
            pl.BlockSpec((None, SIDE_COLS_A, D_MODEL),
                         lambda i: (nxt, jnp.minimum(i, n_a - 1), 0)),
            pl.BlockSpec((pl.Element(1), pl.Element(SIDE_COLS_B), pl.Element(D_MODEL)),
                         lambda i: (nxt, pl.multiple_of(
                             _A_WIDTH + GATE_RANK + jnp.minimum(i, n_b - 1) * SIDE_COLS_B,
                             V7X_SUBLANES), 0)),
        ]
        extra_out_specs += [
            pl.BlockSpec((D_MODEL, SIDE_COLS_A), lambda i: (0, jnp.minimum(i, n_a - 1))),
            pl.BlockSpec((D_MODEL, SIDE_COLS_B), lambda i: (0, jnp.minimum(i, n_b - 1))),
        ]
        extra_out_shapes += [jax.ShapeDtypeStruct((D_MODEL, _A_WIDTH + _FZ_PAD), jnp.bfloat16),
                             jax.ShapeDtypeStruct((D_MODEL, _B_WIDTH), jnp.bfloat16)]
        extra_in += [w_in_t, w_in_t]
        job_bytes += (SIDE_COLS_A + SIDE_COLS_B) * D_MODEL * 6
        for w in (w_oa, w_ob, w_o):
            i_spec, o_spec, o_shape = _cast_job(w, nxt, n_steps, lambda i: i)
            extra_in.append(w)
            extra_in_specs.append(i_spec)
            extra_out_specs.append(o_spec)
            extra_out_shapes.append(o_shape)
            job_bytes += _block_bytes(i_spec, jnp.float32) + _block_bytes(o_spec, jnp.bfloat16)
    return pl.pallas_call(
        functools.partial(_ffn_kernel, layer=layer, final_norm=final_norm),
        grid=(n_steps,),
        in_specs=[tile] + specs + extra_in_specs,
        out_specs=[tile] + extra_out_specs,
        out_shape=[jax.ShapeDtypeStruct(x2d.shape, x2d.dtype)] + extra_out_shapes,
        compiler_params=pltpu.CompilerParams(
            dimension_semantics=("arbitrary",),
            vmem_limit_bytes=_vmem_limit(resident_bytes, 2 * rows * D_MODEL * 4 + job_bytes)),
        name="ffn_final" if final_norm else "ffn",
    )(x2d, *operands, *extra_in)


def _transposed_columns(w_t, layer, start, width, blk):
    _, _, n = w_t.shape
    assert width % blk == 0 and start % V7X_SUBLANES == 0 and blk % V7X_LANES == 0
    return pl.pallas_call(
        _transpose_cast_kernel,
        grid=(width // blk,),
        in_specs=[pl.BlockSpec((pl.Element(1), pl.Element(blk), pl.Element(n)),
                               lambda j: (layer, pl.multiple_of(start + j * blk, V7X_SUBLANES), 0))],
        out_specs=pl.BlockSpec((n, blk), lambda j: (0, j)),
        out_shape=jax.ShapeDtypeStruct((n, width), jnp.bfloat16),
        compiler_params=pltpu.CompilerParams(
            dimension_semantics=("arbitrary",),
            vmem_limit_bytes=_vmem_limit(0, n * blk * 6)),
        name="transpose_cast",
    )(w_t)


def kernel(x, norm1_g, w_in, w_fg2, b_fg, gla_norm_g, w_oa, conv_w, conv_b, w_ob, w_o,
           norm2_g, w_ffn_gate, w_ffn_up, w_ffn_down, final_g):
    b_sz, s_len, d = x.shape
    depth = w_in.shape[0]
    bf16 = jnp.bfloat16
    w_in_t = jnp.swapaxes(w_in, 1, 2)
    wa = _transposed_columns(w_in_t, 0, 0, _A_WIDTH + _FZ_PAD, PREP_COLS_A)
    wb = _transposed_columns(w_in_t, 0, _A_WIDTH + GATE_RANK, _B_WIDTH, PREP_COLS_B)
    woa, wob, wo = w_oa[0].astype(bf16), w_ob[0].astype(bf16), w_o[0].astype(bf16)
    wfg2 = jnp.pad(w_fg2, ((0, 0), (0, _FZ_PAD - GATE_RANK), (0, 0))).astype(bf16)
    gf = final_g[None]
    for l in range(depth):
        x, wg, wu, wd = _mixer(x, l, norm1_g, wa, wb, wfg2, b_fg, gla_norm_g, woa,
                               conv_w, conv_b, wob, wo,
                               (w_ffn_gate, w_ffn_up, w_ffn_down))
        last = l == depth - 1
        outs = _ffn(x.reshape(b_sz * s_len, d), l, norm2_g, wg, wu, wd, gf, final_norm=last,
                    next_f32=None if last else (w_in_t, w_oa, w_ob, w_o))
        x = outs[0].reshape(b_sz, s_len, d)
        if not last:
            wa, wb, woa, wob, wo = outs[1:]
    return x
```
